```python
import math
import jax, jax.numpy as jnp
from jax import lax
import numpy as np

D_MODEL = 1024
BATCH = 1
SEQ = 16384
DEPTH = 1
DEC_BATCH = 8
DEC_SEQ = 8192
PAST_LEN = 128

POOL_WINDOWS = (2, 4, 8, 16)
POOL_GROUP = 64
POOL_WIDTH = POOL_GROUP * len(POOL_WINDOWS)
HEAD_DIM = 64
ATTN_CONFIGS = ((128, 1), (512, 4), (2048, 16))
HEADS_PER_CONFIG = 4
N_ATTN_HEADS = HEADS_PER_CONFIG * len(ATTN_CONFIGS)
ATTN_WIDTH = N_ATTN_HEADS * HEAD_DIM
MIX_WIDTH = POOL_WIDTH + ATTN_WIDTH
IN_WIDTH = POOL_WIDTH + 3 * ATTN_WIDTH
OUT_IN_WIDTH = POOL_WIDTH + HEADS_PER_CONFIG * HEAD_DIM
ROPE_THETA = 10000.0
N_EXPERTS = 16
CAPACITY_FACTOR = 2
EXPERT_FF = 2816
N_MOD = 6
RMS_EPS = 1e-6
NEG_INF = -1e30

kernel_name = "hybrid_pool_dilated_attn_ec_moe_encoder"


def rmsnorm(x, g):
    xf = x.astype(jnp.float32)
    y = xf * lax.rsqrt(jnp.mean(xf * xf, axis=-1, keepdims=True) + RMS_EPS)
    return (y * g.astype(jnp.float32)).astype(x.dtype)


def modulate(h, shift, scale):
    return h * (1.0 + scale[:, None, :]) + shift[:, None, :]


def rope(x):
    S, Dh = x.shape[1], x.shape[3]
    half = Dh // 2
    inv_freq = ROPE_THETA ** (-jnp.arange(half, dtype=jnp.float32) / half)
    ang = jnp.arange(S, dtype=jnp.float32)[:, None] * inv_freq[None, :]
    cos = jnp.cos(ang)[None, :, None, :]
    sin = jnp.sin(ang)[None, :, None, :]
    x1, x2 = x[..., :half], x[..., half:]
    return jnp.concatenate([x1 * cos - x2 * sin, x2 * cos + x1 * sin], axis=-1)


def multi_scale_pool(u, pool_w, pool_scale):
    B, S, _ = u.shape
    uf = u.astype(jnp.float32)
    cs = jnp.concatenate([jnp.zeros((B, 1, POOL_WIDTH), jnp.float32), jnp.cumsum(uf, axis=1)], axis=1)
    t = jnp.arange(S)
    outs = []
    for g, w in enumerate(POOL_WINDOWS):
        sl = slice(g * POOL_GROUP, (g + 1) * POOL_GROUP)
        lo = jnp.clip(t - w // 2, 0, S)
        hi = jnp.clip(t + w // 2, 0, S)
        csg = cs[..., sl]
        mean = (jnp.take(csg, hi, axis=1) - jnp.take(csg, lo, axis=1)) / (hi - lo).astype(jnp.float32)[None, :, None]
        outs.append(jnp.einsum('bsc,cd->bsd', mean - uf[..., sl], pool_w[g].astype(jnp.float32)))
    y = jnp.concatenate(outs, axis=-1) * pool_scale.astype(jnp.float32)
    return y.astype(u.dtype)


def dilated_window_attention(q, k, v, dilation, radius):
    B, S, H, Dh = q.shape
    L = S // dilation
    blk = radius
    nb = -(-L // blk)
    Lp = nb * blk

    def residue(t):
        return t.reshape(B, L, dilation, H, Dh).transpose(0, 2, 1, 3, 4)

    qr = jnp.pad(residue(q), ((0, 0), (0, 0), (0, Lp - L), (0, 0), (0, 0))).reshape(B, dilation, nb, blk, H, Dh)

    def banded(t):
        tp = jnp.pad(residue(t), ((0, 0), (0, 0), (blk, Lp - L + blk), (0, 0), (0, 0)))
        tp = tp.reshape(B, dilation, nb + 2, blk, H, Dh)
        return jnp.concatenate([tp[:, :, :-2], tp[:, :, 1:-1], tp[:, :, 2:]], axis=3)

    kb, vb = banded(k), banded(v)
    s = jnp.einsum('bdnqhe,bdnkhe->bdnhqk', qr, kb) * (1.0 / math.sqrt(Dh))
    qpos = jnp.arange(nb)[:, None, None] * blk + jnp.arange(blk)[None, :, None]
    kpos = jnp.arange(nb)[:, None, None] * blk - blk + jnp.arange(3 * blk)[None, None, :]
    valid = (jnp.abs(kpos - qpos) <= radius) & (kpos >= 0) & (kpos < L)
    s = jnp.where(valid[None, None, :, None], s, NEG_INF)
    m = jnp.max(s, axis=-1, keepdims=True)
    e = jnp.exp(s - m)
    l = jnp.sum(e, axis=-1, keepdims=True)
    o = jnp.einsum('bdnhqk,bdnkhe->bdnqhe', e, vb) / l.transpose(0, 1, 2, 4, 3, 5)
    lse = (m + jnp.log(l))[..., 0].transpose(0, 1, 2, 4, 3)

    def unresidue(t):
        t = t.reshape((B, dilation, Lp) + t.shape[4:])[:, :, :L]
        t = jnp.moveaxis(t, 1, 2)
        return t.reshape((B, S) + t.shape[3:])

    return unresidue(o), unresidue(lse)


def dilated_mixture_attention(q, k, v):
    B, S = q.shape[0], q.shape[1]
    qf = rope(q.astype(jnp.float32))
    kf = rope(k.astype(jnp.float32))
    vf = v.astype(jnp.float32)
    outs, lses = [], []
    for g, (window, dil) in enumerate(ATTN_CONFIGS):
        hs = slice(g * HEADS_PER_CONFIG, (g + 1) * HEADS_PER_CONFIG)
        o, lse = dilated_window_attention(qf[:, :, hs], kf[:, :, hs], vf[:, :, hs], dil, window // (2 * dil))
        outs.append(o)
        lses.append(lse)
    wts = jax.nn.softmax(jnp.stack(lses, axis=0), axis=0)[..., None]
    o = jnp.sum(wts * jnp.stack(outs, axis=0), axis=0)
    return o.reshape(B, S, HEADS_PER_CONFIG * HEAD_DIM)


def expert_choice_ffn(h, w_router, w_gate, w_up, w_down):
    B, S, D = h.shape
    N = B * S
    cap = CAPACITY_FACTOR * N // N_EXPERTS
    hf = h.reshape(N, D)
    aff = jax.nn.softmax(jnp.einsum('nd,de->ne', hf, w_router).astype(jnp.float32), axis=-1)
    gates, idx = lax.top_k(aff.T, cap)
    xe = jnp.take(hf, idx, axis=0)
    hid = jax.nn.silu(jnp.einsum('ecd,edf->ecf', xe, w_gate)) * jnp.einsum('ecd,edf->ecf', xe, w_up)
    ye = (jnp.einsum('ecf,efd->ecd', hid, w_down) * gates[..., None].astype(hid.dtype)).astype(h.dtype)
    out = jnp.zeros((N, D), h.dtype).at[idx.reshape(-1)].add(ye.reshape(-1, D))
    return out.reshape(B, S, D)


def encoder_layer(x, c, norm1_g, norm2_g, w_ada, b_ada, w_in, pool_w, pool_scale, w_out,
                  w_router, w_gate, w_up, w_down):
    B, S, _ = x.shape
    mod = jnp.einsum('bd,dm->bm', jax.nn.silu(c), w_ada) + b_ada
    shift1, scale1, gate1, shift2, scale2, gate2 = jnp.split(mod, N_MOD, axis=-1)

    h = modulate(rmsnorm(x, norm1_g), shift1, scale1)
    proj = jnp.einsum('bsd,dk->bsk', h, w_in)
    u = proj[..., :POOL_WIDTH]
    q, k, v = jnp.split(proj[..., POOL_WIDTH:], 3, axis=-1)
    q = q.reshape(B, S, N_ATTN_HEADS, HEAD_DIM)
    k = k.reshape(B, S, N_ATTN_HEADS, HEAD_DIM)
    v = v.reshape(B, S, N_ATTN_HEADS, HEAD_DIM)
    pool_out = multi_scale_pool(u, pool_w, pool_scale)
    attn_out = dilated_mixture_attention(q, k, v).astype(x.dtype)
    mixed = jnp.einsum('bsk,kd->bsd', jnp.concatenate([pool_out, attn_out], axis=-1), w_out)
    x = x + gate1[:, None, :] * mixed

    h = modulate(rmsnorm(x, norm2_g), shift2, scale2)
    x = x + gate2[:, None, :] * expert_choice_ffn(h, w_router, w_gate, w_up, w_down)
    return x


def trunk(x, c, norm1_g, norm2_g, normf_g, w_ada, b_ada, w_in, pool_w, pool_scale, w_out,
          w_router, w_gate, w_up, w_down):
    for layer in range(DEPTH):
        x = encoder_layer(x, c, norm1_g[layer], norm2_g[layer], w_ada[layer], b_ada[layer],
                          w_in[layer], pool_w[layer], pool_scale[layer], w_out[layer],
                          w_router[layer], w_gate[layer], w_up[layer], w_down[layer])
    return rmsnorm(x, normf_g)


def setup_inputs(seed: int = 0) -> dict:
    key = jax.random.key(seed)
    ks = jax.random.split(key, 17)
    f32 = jnp.float32
    nrm = lambda k, shape: jax.random.normal(k, shape, f32)
    return {
        "x_prompt": nrm(ks[0], (BATCH, SEQ, D_MODEL)),
        "x_sample": nrm(ks[1], (DEC_BATCH, DEC_SEQ, D_MODEL)),
        "c_prompt": nrm(ks[2], (BATCH, D_MODEL)),
        "c_sample": nrm(ks[3], (DEC_BATCH, D_MODEL)),
        "norm1_g": 1.0 + 0.02 * nrm(ks[4], (DEPTH, D_MODEL)),
        "norm2_g": 1.0 + 0.02 * nrm(ks[5], (DEPTH, D_MODEL)),
        "normf_g": 1.0 + 0.02 * nrm(ks[6], (D_MODEL,)),
        "w_ada": 0.02 * nrm(ks[7], (DEPTH, D_MODEL, N_MOD * D_MODEL)),
        "b_ada": 0.01 * nrm(ks[8], (DEPTH, N_MOD * D_MODEL)),
        "w_in": nrm(ks[9], (DEPTH, D_MODEL, IN_WIDTH)) * D_MODEL ** -0.5,
        "pool_w": nrm(ks[10], (DEPTH, len(POOL_WINDOWS), POOL_GROUP, POOL_GROUP)) * POOL_GROUP ** -0.5,
        "pool_scale": 1.0 + 0.02 * nrm(ks[11], (DEPTH, POOL_WIDTH)),
        "w_out": nrm(ks[12], (DEPTH, OUT_IN_WIDTH, D_MODEL)) * OUT_IN_WIDTH ** -0.5,
        "w_router": nrm(ks[13], (DEPTH, D_MODEL, N_EXPERTS)) * D_MODEL ** -0.5,
        "w_gate": nrm(ks[14], (DEPTH, N_EXPERTS, D_MODEL, EXPERT_FF)) * D_MODEL ** -0.5,
        "w_up": nrm(ks[15], (DEPTH, N_EXPERTS, D_MODEL, EXPERT_FF)) * D_MODEL ** -0.5,
        "w_down": nrm(ks[16], (DEPTH, N_EXPERTS, EXPERT_FF, D_MODEL)) * EXPERT_FF ** -0.5,
    }


def reference(x_prompt, x_sample, c_prompt, c_sample, norm1_g, norm2_g, normf_g, w_ada, b_ada,
              w_in, pool_w, pool_scale, w_out, w_router, w_gate, w_up, w_down):
    y_prompt = trunk(x_prompt, c_prompt, norm1_g, norm2_g, normf_g, w_ada, b_ada, w_in, pool_w,
                     pool_scale, w_out, w_router, w_gate, w_up, w_down)
    y_sample = trunk(x_sample, c_sample, norm1_g, norm2_g, normf_g, w_ada, b_ada, w_in, pool_w,
                     pool_scale, w_out, w_router, w_gate, w_up, w_down)
    return (y_prompt, y_sample)
```

```python
import functools
import math

import numpy as np
import jax
import jax.numpy as jnp
from jax import lax
from jax.experimental import pallas as pl
from jax.experimental.pallas import tpu as pltpu

F32 = jnp.float32
BF16 = jnp.bfloat16
I32 = jnp.int32

D_MODEL = 1024
POOL_WINDOWS = (2, 4, 8, 16)
POOL_GROUP = 64
POOL_WIDTH = POOL_GROUP * len(POOL_WINDOWS)
HEAD_DIM = 64
HALF = HEAD_DIM // 2
ATTN_CONFIGS = ((128, 1), (512, 4), (2048, 16))
HEADS_PER_CONFIG = 4
CFG_WIDTH = HEADS_PER_CONFIG * HEAD_DIM
N_CFG = len(ATTN_CONFIGS)
ATTN_WIDTH = N_CFG * CFG_WIDTH
IN_WIDTH = POOL_WIDTH + 3 * ATTN_WIDTH
RADIUS = 64
ROPE_THETA = 10000.0
N_EXPERTS = 16
CAPACITY_FACTOR = 2
EXPERT_FF = 2816
N_MOD = 6
RMS_EPS = 1e-6
NEG_INF = -1e30

LANES = 128
VMEM_LIMIT = 56 * 1024 * 1024

TOKEN_TILE = 512
QUERY_BLOCK = 128
FF_TILE = 256
MOE_ROWS = 2048
VALID_BIT = 1 << 20

assert all(w // (2 * d) == RADIUS for w, d in ATTN_CONFIGS)


def _cparams(sem):
    return pltpu.CompilerParams(dimension_semantics=sem, vmem_limit_bytes=VMEM_LIMIT)


def _ada_kernel(c_ref, w_ref, b_ref, o_ref):
    c = c_ref[...]
    a = c * jax.nn.sigmoid(c)
    o_ref[...] = jnp.dot(a, w_ref[...], preferred_element_type=F32,
                         precision=lax.Precision.HIGHEST) + b_ref[...]


def _ada(c, w_ada, b_ada):
    B = c.shape[0]
    rows = -(-B // 8) * 8
    cp = jnp.pad(c, ((0, rows - B), (0, 0)))
    width = N_MOD * D_MODEL
    tile = 768
    out = pl.pallas_call(
        _ada_kernel,
        out_shape=jax.ShapeDtypeStruct((rows, width), F32),
        grid=(width // tile,),
        in_specs=[pl.BlockSpec((rows, D_MODEL), lambda j: (0, 0)),
                  pl.BlockSpec((D_MODEL, tile), lambda j: (0, j)),
                  pl.BlockSpec((1, tile), lambda j: (0, j))],
        out_specs=pl.BlockSpec((rows, tile), lambda j: (0, j)),
        compiler_params=_cparams(("arbitrary",)),
        name="ada",
    )(cp, w_ada, b_ada.reshape(1, width))
    return out[:B].reshape(B, N_MOD, D_MODEL)


def _rms_mod(x, g, shift, scale):
    y = x * lax.rsqrt(jnp.mean(x * x, axis=-1, keepdims=True) + RMS_EPS)
    return (y * g) * (1.0 + scale) + shift


def _inproj_kernel(x_ref, mod_ref, g_ref, w_ref, cos_ref, sin_ref, u_ref, q_ref, k_ref, v_ref):
    h = _rms_mod(x_ref[...], g_ref[...], mod_ref[0:1, :], mod_ref[1:2, :])
    proj = jnp.dot(h.astype(BF16), w_ref[...], preferred_element_type=F32)
    u_ref[...] = proj[:, :POOL_WIDTH]
    cos = cos_ref[...]
    sin = sin_ref[...]
    for base, ref, scl in ((POOL_WIDTH, q_ref, 1.0 / math.sqrt(HEAD_DIM)), (POOL_WIDTH + ATTN_WIDTH, k_ref, 1.0)):
        for g in range(N_CFG):
            off = base + g * CFG_WIDTH
            x1 = proj[:, off:off + LANES]
            x2 = proj[:, off + LANES:off + CFG_WIDTH]
            ref[:, g * CFG_WIDTH:g * CFG_WIDTH + LANES] = ((x1 * cos - x2 * sin) * scl).astype(BF16)
            ref[:, g * CFG_WIDTH + LANES:(g + 1) * CFG_WIDTH] = ((x2 * cos + x1 * sin) * scl).astype(BF16)
    v_ref[...] = proj[:, POOL_WIDTH + 2 * ATTN_WIDTH:].astype(BF16)


def _qk_column_order():
    cols = list(range(POOL_WIDTH))
    for part in range(2):
        base = POOL_WIDTH + part * ATTN_WIDTH
        for g in range(N_CFG):
            for half in range(2):
                for j in range(HEADS_PER_CONFIG):
                    start = base + (g * HEADS_PER_CONFIG + j) * HEAD_DIM + half * HALF
                    cols.extend(range(start, start + HALF))
    cols.extend(range(POOL_WIDTH + 2 * ATTN_WIDTH, IN_WIDTH))
    return np.asarray(cols, dtype=np.int32)


def _rope_tables(S):
    inv_freq = ROPE_THETA ** (-jnp.arange(HALF, dtype=F32) / HALF)
    ang = jnp.arange(S, dtype=F32)[:, None] * inv_freq[None, :]
    reps = LANES // HALF
    return jnp.tile(jnp.cos(ang), (1, reps)), jnp.tile(jnp.sin(ang), (1, reps))


def _in_proj(x, mod, g1, w_in_b, cos, sin):
    B, S, _ = x.shape
    tm = TOKEN_TILE
    tok = lambda w: pl.BlockSpec((None, tm, w), lambda b, i: (b, i, 0))
    return pl.pallas_call(
        _inproj_kernel,
        out_shape=(jax.ShapeDtypeStruct((B, S, POOL_WIDTH), F32),
                   jax.ShapeDtypeStruct((B, S, ATTN_WIDTH), BF16),
                   jax.ShapeDtypeStruct((B, S, ATTN_WIDTH), BF16),
                   jax.ShapeDtypeStruct((B, S, ATTN_WIDTH), BF16)),
        grid=(B, S // tm),
        in_specs=[tok(D_MODEL),
                  pl.BlockSpec((None, N_MOD, D_MODEL), lambda b, i: (b, 0, 0)),
                  pl.BlockSpec((1, D_MODEL), lambda b, i: (0, 0)),
                  pl.BlockSpec((D_MODEL, IN_WIDTH), lambda b, i: (0, 0)),
                  pl.BlockSpec((tm, LANES), lambda b, i: (i, 0)),
                  pl.BlockSpec((tm, LANES), lambda b, i: (i, 0))],
        out_specs=(tok(POOL_WIDTH), tok(ATTN_WIDTH), tok(ATTN_WIDTH), tok(ATTN_WIDTH)),
        compiler_params=_cparams(("parallel", "parallel")),
        name="in_proj",
    )(x, mod, g1, w_in_b, cos, sin)


def _attn_kernel(q_ref, kp_ref, kc_ref, kn_ref, vp_ref, vc_ref, vn_ref, o_ref, lse_ref, *, seq_len):
    i = pl.program_id(1)
    qb = QUERY_BLOCK
    nk = qb + 2 * RADIUS
    q = q_ref[...]
    lane = lax.broadcasted_iota(I32, (qb, CFG_WIDTH), 1)
    q_head = (lane % LANES) // HALF
    v_head = lane // HEAD_DIM
    zero = jnp.zeros_like(q)
    q_stack = jnp.concatenate([jnp.where(q_head == j, q, zero) for j in range(HEADS_PER_CONFIG)], axis=0)
    k = jnp.concatenate([kp_ref[...], kc_ref[...], kn_ref[...]], axis=0)
    v = jnp.concatenate([vp_ref[...], vc_ref[...], vn_ref[...]], axis=0)
    s = lax.dot_general(q_stack, k, (((1,), (1,)), ((), ())), preferred_element_type=F32)
    rows = HEADS_PER_CONFIG * qb
    qi = lax.broadcasted_iota(I32, (rows, nk), 0) % qb
    kj = lax.broadcasted_iota(I32, (rows, nk), 1)
    kpos = i * qb - RADIUS + kj
    valid = (jnp.abs(kj - RADIUS - qi) <= RADIUS) & (kpos >= 0) & (kpos < seq_len)
    s = jnp.where(valid, s, NEG_INF)
    m = jnp.max(s, axis=-1, keepdims=True)
    e = jnp.exp(s - m)
    l = jnp.sum(e, axis=-1, keepdims=True)
    r = jnp.dot(e.astype(BF16), v, preferred_element_type=F32) / l
    lse = jnp.broadcast_to(m + jnp.log(l), (rows, CFG_WIDTH))
    o = jnp.zeros((qb, CFG_WIDTH), F32)
    lse_o = jnp.zeros((qb, CFG_WIDTH), F32)
    for j in range(HEADS_PER_CONFIG):
        o = jnp.where(v_head == j, r[j * qb:(j + 1) * qb], o)
        lse_o = jnp.where(v_head == j, lse[j * qb:(j + 1) * qb], lse_o)
    o_ref[...] = o
    lse_ref[...] = lse_o


def _attention(q, k, v, lane_block):
    nseq, L, _ = q.shape
    qb = QUERY_BLOCK
    per = qb // RADIUS
    last = L // RADIUS - 1
    cur = pl.BlockSpec((None, qb, CFG_WIDTH), lambda b, i: (b, i, lane_block))
    prev = pl.BlockSpec((None, RADIUS, CFG_WIDTH), lambda b, i: (b, jnp.maximum(i * per - 1, 0), lane_block))
    nxt = pl.BlockSpec((None, RADIUS, CFG_WIDTH), lambda b, i: (b, jnp.minimum((i + 1) * per, last), lane_block))
    out = pl.BlockSpec((None, qb, CFG_WIDTH), lambda b, i: (b, i, 0))
    return pl.pallas_call(
        functools.partial(_attn_kernel, seq_len=L),
        out_shape=(jax.ShapeDtypeStruct((nseq, L, CFG_WIDTH), F32),
                   jax.ShapeDtypeStruct((nseq, L, CFG_WIDTH), F32)),
        grid=(nseq, L // qb),
        in_specs=[cur, prev, cur, nxt, prev, cur, nxt],
        out_specs=(out, out),
        compiler_params=_cparams(("parallel", "parallel")),
        name="banded_attention",
    )(q, k, k, k, v, v, v)


def _to_residues(t, g, dil):
    B, S, _ = t.shape
    t = t[:, :, g * CFG_WIDTH:(g + 1) * CFG_WIDTH].reshape(B, S // dil, dil, CFG_WIDTH)
    return t.transpose(0, 2, 1, 3).reshape(B * dil, S // dil, CFG_WIDTH)


def _from_residues(t, B, dil):
    _, L, W = t.shape
    return t.reshape(B, dil, L, W).transpose(0, 2, 1, 3).reshape(B, L * dil, W)


def _mix_kernel(x_ref, mod_ref, g2_ref, up_ref, uc_ref, un_ref, o0_ref, l0_ref, o1_ref, l1_ref, o2_ref, l2_ref,
                pw_ref, ps_ref, wout_ref, wr_ref, x1_ref, h_ref, aff_ref, *, seq_len):
    i = pl.program_id(1)
    tm = TOKEN_TILE
    halo = POOL_WINDOWS[-1] // 2
    u = uc_ref[...]
    up = jnp.where(i > 0, up_ref[...], 0.0)
    un = jnp.where(i < pl.num_programs(1) - 1, un_ref[...], 0.0)
    ext = jnp.concatenate([up, u, un], axis=0)
    p2 = ext[:-1] + ext[1:]
    q4 = p2[:-2] + p2[2:]
    q8 = q4[:-4] + q4[4:]
    q16 = q8[:-8] + q8[8:]
    lane = lax.broadcasted_iota(I32, (tm, POOL_WIDTH), 1)
    grp = lane // POOL_GROUP
    wsum = jnp.where(grp == 0, p2[7:7 + tm],
                     jnp.where(grp == 1, q4[6:6 + tm], jnp.where(grp == 2, q8[4:4 + tm], q16[0:tm])))
    half_w = jnp.left_shift(1, grp)
    t = i * tm + lax.broadcasted_iota(I32, (tm, POOL_WIDTH), 0)
    cnt = jnp.minimum(t + half_w, seq_len) - jnp.maximum(t - half_w, 0)
    diff = wsum / cnt.astype(F32) - u
    pooled = jnp.dot(diff.astype(BF16), pw_ref[...], preferred_element_type=F32) * ps_ref[...]

    lses = (l0_ref[...], l1_ref[...], l2_ref[...])
    outs = (o0_ref[...], o1_ref[...], o2_ref[...])
    mx = jnp.maximum(jnp.maximum(lses[0], lses[1]), lses[2])
    ws = [jnp.exp(l - mx) for l in lses]
    attn = (ws[0] * outs[0] + ws[1] * outs[1] + ws[2] * outs[2]) / (ws[0] + ws[1] + ws[2])

    cat = jnp.concatenate([pooled, attn], axis=-1).astype(BF16)
    mixed = jnp.dot(cat, wout_ref[...], preferred_element_type=F32)
    x1 = x_ref[...] + mod_ref[2:3, :] * mixed
    x1_ref[...] = x1
    h = _rms_mod(x1, g2_ref[...], mod_ref[3:4, :], mod_ref[4:5, :])
    h_ref[...] = h
    logits = lax.dot_general(wr_ref[...], h, (((1,), (1,)), ((), ())), preferred_element_type=F32,
                             precision=lax.Precision.HIGHEST)
    z = jnp.exp(logits - jnp.max(logits, axis=0, keepdims=True))
    aff_ref[...] = z / jnp.sum(z, axis=0, keepdims=True)


def _mix(x, mod, g2, u, attn_parts, pool_w_bd, pool_scale, w_out_b, w_router_t):
    B, S, _ = x.shape
    tm = TOKEN_TILE
    halo = POOL_WINDOWS[-1] // 2
    per = tm // halo
    last = S // halo - 1
    nt = S // tm
    tok = lambda w: pl.BlockSpec((None, tm, w), lambda b, i: (b, i, 0))
    const = lambda shape: pl.BlockSpec(shape, lambda b, i: (0,) * len(shape))
    in_specs = [tok(D_MODEL),
                pl.BlockSpec((None, N_MOD, D_MODEL), lambda b, i: (b, 0, 0)),
                const((1, D_MODEL)),
                pl.BlockSpec((None, halo, POOL_WIDTH), lambda b, i: (b, jnp.maximum(i * per - 1, 0), 0)),
                tok(POOL_WIDTH),
                pl.BlockSpec((None, halo, POOL_WIDTH), lambda b, i: (b, jnp.minimum((i + 1) * per, last), 0))]
    in_specs += [tok(CFG_WIDTH)] * (2 * N_CFG)
    in_specs += [const((POOL_WIDTH, POOL_WIDTH)), const((1, POOL_WIDTH)),
                 const((POOL_WIDTH + CFG_WIDTH, D_MODEL)), const((N_EXPERTS, D_MODEL))]
    flat = [a for pair in attn_parts for a in pair]
    return pl.pallas_call(
        functools.partial(_mix_kernel, seq_len=S),
        out_shape=(jax.ShapeDtypeStruct((B, S, D_MODEL), F32),
                   jax.ShapeDtypeStruct((B, S, D_MODEL), F32),
                   jax.ShapeDtypeStruct((N_EXPERTS, B * S), F32)),
        grid=(B, nt),
        in_specs=in_specs,
        out_specs=(tok(D_MODEL), tok(D_MODEL),
                   pl.BlockSpec((N_EXPERTS, tm), lambda b, i: (0, b * nt + i))),
        compiler_params=_cparams(("parallel", "parallel")),
        name="mix_router",
    )(x, mod, g2, u, u, u, *flat, pool_w_bd, pool_scale, w_out_b, w_router_t)


def _total(x):
    return jnp.sum(jnp.sum(x, axis=0, keepdims=True), axis=1, keepdims=True)


def _flat_prefix_excl(mask, upper, lower_strict):
    xb = jnp.where(mask, 1.0, 0.0).astype(BF16)
    incl = jnp.dot(xb, upper, preferred_element_type=F32)
    above = jnp.sum(jnp.dot(lower_strict, xb, preferred_element_type=F32), axis=1, keepdims=True)
    return incl - xb.astype(F32) + above


def _route_kernel(a_ref, idx_ref, gate_ref, *, capacity):
    a = a_ref[...]
    rows = a.shape[0]
    n = rows * LANES
    bits = pltpu.bitcast(a, I32)
    cap = jnp.float32(capacity)
    thr = jnp.zeros((1, 1), I32)
    for b in range(30, -1, -1):
        cand = thr | (1 << b)
        cnt = _total(jnp.where(bits >= cand, 1.0, 0.0))
        thr = jnp.where(cnt >= cap, cand, thr)
    gt = bits > thr
    eq = bits == thr
    need = cap - _total(jnp.where(gt, 1.0, 0.0))

    li = lax.broadcasted_iota(I32, (LANES, LANES), 0)
    lj = lax.broadcasted_iota(I32, (LANES, LANES), 1)
    upper = jnp.where(li <= lj, 1.0, 0.0).astype(BF16)
    ri = lax.broadcasted_iota(I32, (rows, rows), 0)
    rj = lax.broadcasted_iota(I32, (rows, rows), 1)
    lower_strict = jnp.where(rj < ri, 1.0, 0.0).astype(BF16)

    sel = gt | (eq & (_flat_prefix_excl(eq, upper, lower_strict) < need))
    pos = _flat_prefix_excl(sel, upper, lower_strict).astype(I32)
    lane = lax.broadcasted_iota(I32, (rows, LANES), 1)
    flat = lax.broadcasted_iota(I32, (rows, LANES), 0) * LANES + lane
    d = jnp.where(sel, (flat - pos) | VALID_BIT, 0)
    g = a
    for b in range((n - 1).bit_length()):
        if (1 << b) < LANES:
            s = 1 << b
            keep = lane < LANES - s

            def shift(x, s=s, keep=keep):
                y = pltpu.roll(x, LANES - s, axis=1)
                return jnp.where(keep, y, pltpu.roll(y, rows - 1, axis=0))
        else:
            k = (1 << b) // LANES

            def shift(x, k=k):
                return pltpu.roll(x, rows - k, axis=0)
        d_in = shift(d)
        g_in = shift(g)
        arrive = ((d_in >> b) & 1) == 1
        leave = ((d >> b) & 1) == 1
        g = jnp.where(arrive, g_in, g)
        d = jnp.where(arrive, d_in, jnp.where(leave, 0, d))
    out_rows = capacity // LANES
    idx_ref[...] = (flat + (d & (VALID_BIT - 1)))[:out_rows]
    gate_ref[...] = g[:out_rows]


def _route(aff_t, capacity):
    n = aff_t.shape[1]
    rows = n // LANES
    out_rows = capacity // LANES
    a3 = aff_t.reshape(N_EXPERTS, rows, LANES)
    return pl.pallas_call(
        functools.partial(_route_kernel, capacity=capacity),
        out_shape=(jax.ShapeDtypeStruct((N_EXPERTS, out_rows, LANES), I32),
                   jax.ShapeDtypeStruct((N_EXPERTS, out_rows, LANES), F32)),
        grid=(N_EXPERTS,),
        in_specs=[pl.BlockSpec((None, rows, LANES), lambda e: (e, 0, 0))],
        out_specs=(pl.BlockSpec((None, out_rows, LANES), lambda e: (e, 0, 0)),
                   pl.BlockSpec((None, out_rows, LANES), lambda e: (e, 0, 0))),
        compiler_params=_cparams(("parallel",)),
        name="route",
    )(a3)


def _moe_kernel(idx_hbm, gate_ref, h_hbm, wg_ref, wu_ref, wd_ref, acc_in_hbm, out_hbm,
                idx_smem, row_buf, x_buf, acc, sems, *, rows, tiles_per_expert):
    del acc_in_hbm
    e = pl.program_id(0)
    i = pl.program_id(1)
    f = pl.program_id(2)
    tile = e * tiles_per_expert + i

    def row_copies(hbm, to_vmem):
        def issue(r, carry):
            tok = idx_smem[r]
            src, dst = hbm.at[pl.ds(tok, 1)], row_buf.at[pl.ds(r, 1)]
            if not to_vmem:
                src, dst = dst, src
            pltpu.make_async_copy(src, dst, sems.at[1]).start()
            return carry
        lax.fori_loop(0, rows, issue, 0, unroll=8)
        pltpu.make_async_copy(hbm.at[pl.ds(0, rows)], row_buf, sems.at[1]).wait()

    @pl.when(f == 0)
    def _():
        cp = pltpu.make_async_copy(idx_hbm.at[tile], idx_smem, sems.at[0])
        cp.start()
        cp.wait()
        row_copies(h_hbm, True)
        x_buf[...] = row_buf[...].astype(BF16)
        acc[...] = jnp.zeros_like(acc)

    x = x_buf[...]
    gate = jnp.dot(x, wg_ref[...].astype(BF16), preferred_element_type=F32)
    up = jnp.dot(x, wu_ref[...].astype(BF16), preferred_element_type=F32)
    hid = (gate * jax.nn.sigmoid(gate) * up).astype(BF16)
    acc[...] += jnp.dot(hid, wd_ref[...].astype(BF16), preferred_element_type=F32)

    @pl.when(f == pl.num_programs(2) - 1)
    def _():
        row_copies(out_hbm, True)
        ri = lax.broadcasted_iota(I32, (LANES, LANES), 0)
        ci = lax.broadcasted_iota(I32, (LANES, LANES), 1)
        for c in range(rows // LANES):
            sl = pl.ds(c * LANES, LANES)
            gr = jnp.broadcast_to(gate_ref[:, sl], (LANES, LANES))
            gcol = jnp.sum(jnp.where(ri == ci, gr, 0.0), axis=1, keepdims=True)
            row_buf[sl, :] += acc[sl, :] * gcol
        row_copies(out_hbm, False)


def _moe_ffn(h, idx, gates, w_gate, w_up, w_down):
    n = h.shape[0]
    capacity = idx.shape[1] * idx.shape[2]
    rows = min(MOE_ROWS, capacity)
    tiles = capacity // rows
    nf = EXPERT_FF // FF_TILE
    idx2 = idx.reshape(N_EXPERTS * tiles, rows)
    gates3 = gates.reshape(N_EXPERTS * tiles, 1, rows)
    return pl.pallas_call(
        functools.partial(_moe_kernel, rows=rows, tiles_per_expert=tiles),
        out_shape=jax.ShapeDtypeStruct((n, D_MODEL), F32),
        grid=(N_EXPERTS, tiles, nf),
        in_specs=[pl.BlockSpec(memory_space=pl.ANY),
                  pl.BlockSpec((None, 1, rows), lambda e, i, f: (e * tiles + i, 0, 0)),
                  pl.BlockSpec(memory_space=pl.ANY),
                  pl.BlockSpec((None, D_MODEL, FF_TILE), lambda e, i, f: (e, 0, f)),
                  pl.BlockSpec((None, D_MODEL, FF_TILE), lambda e, i, f: (e, 0, f)),
                  pl.BlockSpec((None, FF_TILE, D_MODEL), lambda e, i, f: (e, f, 0)),
                  pl.BlockSpec(memory_space=pl.ANY)],
        out_specs=pl.BlockSpec(memory_space=pl.ANY),
        scratch_shapes=[pltpu.SMEM((rows,), I32),
                        pltpu.VMEM((rows, D_MODEL), F32),
                        pltpu.VMEM((rows, D_MODEL), BF16),
                        pltpu.VMEM((rows, D_MODEL), F32),
                        pltpu.SemaphoreType.DMA((2,))],
        input_output_aliases={6: 0},
        compiler_params=_cparams(("arbitrary", "arbitrary", "arbitrary")),
        name="moe_ffn",
    )(idx2, gates3, h, w_gate, w_up, w_down, jnp.zeros((n, D_MODEL), F32))


def _final_kernel(x_ref, moe_ref, mod_ref, g_ref, o_ref):
    x = x_ref[...] + mod_ref[5:6, :] * moe_ref[...]
    o_ref[...] = (x * lax.rsqrt(jnp.mean(x * x, axis=-1, keepdims=True) + RMS_EPS)) * g_ref[...]


def _final(x1, moe, mod, gf):
    B, S, _ = x1.shape
    tm = TOKEN_TILE
    tok = pl.BlockSpec((None, tm, D_MODEL), lambda b, i: (b, i, 0))
    return pl.pallas_call(
        _final_kernel,
        out_shape=jax.ShapeDtypeStruct((B, S, D_MODEL), F32),
        grid=(B, S // tm),
        in_specs=[tok, tok,
                  pl.BlockSpec((None, N_MOD, D_MODEL), lambda b, i: (b, 0, 0)),
                  pl.BlockSpec((1, D_MODEL), lambda b, i: (0, 0))],
        out_specs=tok,
        compiler_params=_cparams(("parallel", "parallel")),
        name="final_norm",
    )(x1, moe, mod, gf)


def _trunk_group(x, c, p):
    B, S, _ = x.shape
    n = B * S
    mod = _ada(c, p["w_ada"], p["b_ada"])
    cos, sin = _rope_tables(S)
    u, q, k, v = _in_proj(x, mod, p["g1"], p["w_in"], cos, sin)
    parts = []
    for g, (_, dil) in enumerate(ATTN_CONFIGS):
        if dil == 1:
            o, lse = _attention(q, k, v, g)
        else:
            o, lse = _attention(_to_residues(q, g, dil), _to_residues(k, g, dil), _to_residues(v, g, dil), 0)
            o, lse = _from_residues(o, B, dil), _from_residues(lse, B, dil)
        parts.append((o, lse))
    x1, h, aff_t = _mix(x, mod, p["g2"], u, parts, p["pool_w"], p["pool_scale"], p["w_out"], p["w_router_t"])
    idx, gates = _route(aff_t, CAPACITY_FACTOR * n // N_EXPERTS)
    moe = _moe_ffn(h.reshape(n, D_MODEL), idx, gates, p["w_gate"], p["w_up"], p["w_down"])
    return _final(x1, moe.reshape(B, S, D_MODEL), mod, p["gf"])


def kernel(x_prompt, x_sample, c_prompt, c_sample, norm1_g, norm2_g, normf_g, w_ada, b_ada, w_in, pool_w,
           pool_scale, w_out, w_router, w_gate, w_up, w_down):
    assert norm1_g.shape[0] == 1, "one encoder layer"
    pool_bd = jax.scipy.linalg.block_diag(*[pool_w[0, g] for g in range(len(POOL_WINDOWS))])
    p = {
        "g1": norm1_g[0].reshape(1, D_MODEL), "g2": norm2_g[0].reshape(1, D_MODEL),
        "gf": normf_g.reshape(1, D_MODEL),
        "w_ada": w_ada[0], "b_ada": b_ada[0],
        "w_in": jnp.take(w_in[0], _qk_column_order(), axis=1).astype(BF16),
        "pool_w": pool_bd.astype(BF16), "pool_scale": pool_scale[0].reshape(1, POOL_WIDTH),
        "w_out": w_out[0].astype(BF16), "w_router_t": w_router[0].T,
        "w_gate": w_gate[0], "w_up": w_up[0], "w_down": w_down[0],
    }
    return (_trunk_group(x_prompt, c_prompt, p), _trunk_group(x_sample, c_sample, p))
```

```python
import functools
import math

import numpy as np
import jax
import jax.numpy as jnp
from jax import lax
from jax.experimental import pallas as pl
from jax.experimental.pallas import tpu as pltpu

F32 = jnp.float32
BF16 = jnp.bfloat16
I32 = jnp.int32

D_MODEL = 1024
POOL_WINDOWS = (2, 4, 8, 16)
POOL_GROUP = 64
POOL_WIDTH = POOL_GROUP * len(POOL_WINDOWS)
HEAD_DIM = 64
HALF = HEAD_DIM // 2
ATTN_CONFIGS = ((128, 1), (512, 4), (2048, 16))
HEADS_PER_CONFIG = 4
CFG_WIDTH = HEADS_PER_CONFIG * HEAD_DIM
N_CFG = len(ATTN_CONFIGS)
ATTN_WIDTH = N_CFG * CFG_WIDTH
IN_WIDTH = POOL_WIDTH + 3 * ATTN_WIDTH
RADIUS = 64
ROPE_THETA = 10000.0
N_EXPERTS = 16
CAPACITY_FACTOR = 2
EXPERT_FF = 2816
N_MOD = 6
RMS_EPS = 1e-6
NEG_INF = -1e30

LANES = 128
SUBLANES = 8
TILE_ROWS = D_MODEL // LANES
VMEM_LIMIT = 56 * 1024 * 1024

TOKEN_TILE = 512
QUERY_BLOCK = 128
ATTN_CHUNK = 512
FF_TILE = 256
MOE_ROWS = 2048
GATHER_CHUNK = 256
VALID_BIT = 1 << 20

assert all(w // (2 * d) == RADIUS for w, d in ATTN_CONFIGS)
assert TILE_ROWS == SUBLANES


def _cparams(sem):
    return pltpu.CompilerParams(dimension_semantics=sem, vmem_limit_bytes=VMEM_LIMIT)


def _ada_kernel(c_ref, w_ref, b_ref, o_ref):
    c = c_ref[...]
    a = c * jax.nn.sigmoid(c)
    o_ref[...] = jnp.dot(a, w_ref[...], preferred_element_type=F32,
                         precision=lax.Precision.HIGHEST) + b_ref[...]


def _ada(c, w_ada, b_ada):
    B = c.shape[0]
    rows = -(-B // SUBLANES) * SUBLANES
    cp = jnp.pad(c, ((0, rows - B), (0, 0)))
    width = N_MOD * D_MODEL
    tile = 768
    out = pl.pallas_call(
        _ada_kernel,
        out_shape=jax.ShapeDtypeStruct((rows, width), F32),
        grid=(width // tile,),
        in_specs=[pl.BlockSpec((rows, D_MODEL), lambda j: (0, 0)),
                  pl.BlockSpec((D_MODEL, tile), lambda j: (0, j)),
                  pl.BlockSpec((1, tile), lambda j: (0, j))],
        out_specs=pl.BlockSpec((rows, tile), lambda j: (0, j)),
        compiler_params=_cparams(("arbitrary",)),
        name="ada",
    )(cp, w_ada, b_ada.reshape(1, width))
    return out[:B].reshape(B, N_MOD, D_MODEL)


def _rms_mod(x, g, shift, scale):
    y = x * lax.rsqrt(jnp.mean(x * x, axis=-1, keepdims=True) + RMS_EPS)
    return (y * g) * (1.0 + scale) + shift


def _inproj_kernel(x_ref, mod_ref, g_ref, w_ref, cos_ref, sin_ref, u_ref, *rest):
    qkv = (rest[0:3], rest[3:6], rest[6:9])
    scr = rest[9]
    tm = TOKEN_TILE
    h = _rms_mod(x_ref[...], g_ref[...], mod_ref[0:1, :], mod_ref[1:2, :])
    proj = jnp.dot(h.astype(BF16), w_ref[...], preferred_element_type=F32)
    u_ref[...] = proj[:, :POOL_WIDTH]
    cos = cos_ref[...]
    sin = sin_ref[...]
    slab = 0
    for g, (_, dil) in enumerate(ATTN_CONFIGS):
        pieces = []
        for part, scl in ((0, 1.0 / math.sqrt(HEAD_DIM)), (1, 1.0)):
            off = POOL_WIDTH + part * ATTN_WIDTH + g * CFG_WIDTH
            x1 = proj[:, off:off + LANES]
            x2 = proj[:, off + LANES:off + CFG_WIDTH]
            pieces.append((qkv[g][part], 0, (x1 * cos - x2 * sin) * scl))
            pieces.append((qkv[g][part], LANES, (x2 * cos + x1 * sin) * scl))
        voff = POOL_WIDTH + 2 * ATTN_WIDTH + g * CFG_WIDTH
        pieces.append((qkv[g][2], 0, proj[:, voff:voff + LANES]))
        pieces.append((qkv[g][2], LANES, proj[:, voff + LANES:voff + CFG_WIDTH]))
        for ref, lo, val in pieces:
            if dil == 1:
                ref[0, :, lo:lo + LANES] = val.astype(BF16)
            else:
                scr[slab] = val
                for r in range(dil):
                    ref[r, :, lo:lo + LANES] = scr[slab, pl.ds(r, tm // dil, stride=dil), :].astype(BF16)
                slab += 1


def _qk_column_order():
    cols = list(range(POOL_WIDTH))
    for part in range(2):
        base = POOL_WIDTH + part * ATTN_WIDTH
        for g in range(N_CFG):
            for half in range(2):
                for j in range(HEADS_PER_CONFIG):
                    start = base + (g * HEADS_PER_CONFIG + j) * HEAD_DIM + half * HALF
                    cols.extend(range(start, start + HALF))
    cols.extend(range(POOL_WIDTH + 2 * ATTN_WIDTH, IN_WIDTH))
    return np.asarray(cols, dtype=np.int32)


def _rope_tables(S):
    inv_freq = ROPE_THETA ** (-jnp.arange(HALF, dtype=F32) / HALF)
    ang = jnp.arange(S, dtype=F32)[:, None] * inv_freq[None, :]
    reps = LANES // HALF
    return jnp.tile(jnp.cos(ang), (1, reps)), jnp.tile(jnp.sin(ang), (1, reps))


def _in_proj(x, mod, g1, w_in_b, cos, sin):
    B, S, _ = x.shape
    tm = TOKEN_TILE
    tok = lambda w: pl.BlockSpec((None, tm, w), lambda b, i: (b, i, 0))
    out_shape = [jax.ShapeDtypeStruct((B, S, POOL_WIDTH), F32)]
    out_specs = [tok(POOL_WIDTH)]
    slabs = 0
    for _, dil in ATTN_CONFIGS:
        assert tm % (dil * 2 * SUBLANES) == 0 and S % (dil * ATTN_CHUNK) == 0
        slabs += 0 if dil == 1 else 3 * CFG_WIDTH // LANES
        for _ in range(3):
            out_shape.append(jax.ShapeDtypeStruct((B, dil, S // dil, CFG_WIDTH), BF16))
            out_specs.append(pl.BlockSpec((None, dil, tm // dil, CFG_WIDTH), lambda b, i: (b, 0, i, 0)))
    return pl.pallas_call(
        _inproj_kernel,
        out_shape=tuple(out_shape),
        grid=(B, S // tm),
        in_specs=[tok(D_MODEL),
                  pl.BlockSpec((None, N_MOD, D_MODEL), lambda b, i: (b, 0, 0)),
                  pl.BlockSpec((1, D_MODEL), lambda b, i: (0, 0)),
                  pl.BlockSpec((D_MODEL, IN_WIDTH), lambda b, i: (0, 0)),
                  pl.BlockSpec((tm, LANES), lambda b, i: (i, 0)),
                  pl.BlockSpec((tm, LANES), lambda b, i: (i, 0))],
        out_specs=tuple(out_specs),
        scratch_shapes=[pltpu.VMEM((slabs, tm, LANES), F32)],
        compiler_params=_cparams(("parallel", "parallel")),
        name="in_proj",
    )(x, mod, g1, w_in_b, cos, sin)


def _band_bias():
    rows = HEADS_PER_CONFIG * QUERY_BLOCK
    nk = QUERY_BLOCK + 2 * RADIUS
    qi = np.arange(rows)[:, None] % QUERY_BLOCK
    kj = np.arange(nk)[None, :]
    band = np.abs(kj - RADIUS - qi) <= RADIUS
    tables = [band, band & (kj >= RADIUS), band & (kj < nk - RADIUS)]
    return np.stack([np.where(t, 0.0, NEG_INF) for t in tables]).astype(np.float32)


def _attn_kernel(q_ref, kp_ref, kc_ref, kn_ref, vp_ref, vc_ref, vn_ref, bias_ref, o_ref, lse_ref):
    i = pl.program_id(2)
    last = pl.num_programs(2) - 1
    qb = QUERY_BLOCK
    nsub = ATTN_CHUNK // qb
    nk = qb + 2 * RADIUS
    rows = HEADS_PER_CONFIG * qb
    lane = lax.broadcasted_iota(I32, (qb, CFG_WIDTH), 1)
    q_head = (lane % LANES) // HALF
    v_head = lane // HEAD_DIM
    k = jnp.concatenate([kp_ref[...], kc_ref[...], kn_ref[...]], axis=0)
    v = jnp.concatenate([vp_ref[...], vc_ref[...], vn_ref[...]], axis=0)
    for j in range(nsub):
        q = q_ref[j * qb:(j + 1) * qb, :]
        zero = jnp.zeros_like(q)
        q_stack = jnp.concatenate([jnp.where(q_head == hd, q, zero) for hd in range(HEADS_PER_CONFIG)], axis=0)
        s = lax.dot_general(q_stack, k[j * qb:j * qb + nk], (((1,), (1,)), ((), ())),
                            preferred_element_type=F32)
        which = 0
        if j == 0:
            which = jnp.where(i == 0, 1, 0)
        if j == nsub - 1:
            which = jnp.where(i == last, 2, 0)
        s = s + bias_ref[which]
        m = jnp.max(s, axis=-1, keepdims=True)
        e = jnp.exp(s - m)
        l = jnp.sum(e, axis=-1, keepdims=True)
        r = jnp.dot(e.astype(BF16), v[j * qb:j * qb + nk], preferred_element_type=F32) / l
        lse = jnp.broadcast_to(m + jnp.log(l), (rows, CFG_WIDTH))
        o = jnp.zeros((qb, CFG_WIDTH), F32)
        lse_o = jnp.zeros((qb, CFG_WIDTH), F32)
        for hd in range(HEADS_PER_CONFIG):
            o = jnp.where(v_head == hd, r[hd * qb:(hd + 1) * qb], o)
            lse_o = jnp.where(v_head == hd, lse[hd * qb:(hd + 1) * qb], lse_o)
        o_ref[j * qb:(j + 1) * qb, :] = o
        lse_ref[j * qb:(j + 1) * qb, :] = lse_o


def _attention(q, k, v, bias):
    B, dil, L, _ = q.shape
    ch = ATTN_CHUNK
    assert ch // QUERY_BLOCK >= 2 and L % ch == 0
    per = ch // RADIUS
    last = L // RADIUS - 1
    cur = pl.BlockSpec((None, None, ch, CFG_WIDTH), lambda b, r, i: (b, r, i, 0))
    prev = pl.BlockSpec((None, None, RADIUS, CFG_WIDTH), lambda b, r, i: (b, r, jnp.maximum(i * per - 1, 0), 0))
    nxt = pl.BlockSpec((None, None, RADIUS, CFG_WIDTH), lambda b, r, i: (b, r, jnp.minimum((i + 1) * per, last), 0))
    out = pl.BlockSpec((None, ch, CFG_WIDTH), lambda b, r, i: (b, i, r))
    return pl.pallas_call(
        _attn_kernel,
        out_shape=(jax.ShapeDtypeStruct((B, L, dil * CFG_WIDTH), F32),
                   jax.ShapeDtypeStruct((B, L, dil * CFG_WIDTH), F32)),
        grid=(B, dil, L // ch),
        in_specs=[cur, prev, cur, nxt, prev, cur, nxt,
                  pl.BlockSpec(bias.shape, lambda b, r, i: (0, 0, 0))],
        out_specs=(out, out),
        compiler_params=_cparams(("parallel", "parallel", "parallel")),
        name="banded_attention",
    )(q, k, k, k, v, v, v, bias)


def _mix_kernel(x_ref, mod_ref, g2_ref, up_ref, uc_ref, un_ref, o0_ref, l0_ref, o1_ref, l1_ref, o2_ref, l2_ref,
                pw_ref, ps_ref, wout_ref, wr_ref, x1_ref, h_ref, aff_ref, scr, *, seq_len):
    i = pl.program_id(1)
    tm = TOKEN_TILE
    u = uc_ref[...]
    up = jnp.where(i > 0, up_ref[...], 0.0)
    un = jnp.where(i < pl.num_programs(1) - 1, un_ref[...], 0.0)
    ext = jnp.concatenate([up, u, un], axis=0)
    p2 = ext[:-1] + ext[1:]
    q4 = p2[:-2] + p2[2:]
    q8 = q4[:-4] + q4[4:]
    q16 = q8[:-8] + q8[8:]
    lane = lax.broadcasted_iota(I32, (tm, POOL_WIDTH), 1)
    grp = lane // POOL_GROUP
    wsum = jnp.where(grp == 0, p2[7:7 + tm],
                     jnp.where(grp == 1, q4[6:6 + tm], jnp.where(grp == 2, q8[4:4 + tm], q16[0:tm])))
    half_w = jnp.left_shift(1, grp)
    t = i * tm + lax.broadcasted_iota(I32, (tm, POOL_WIDTH), 0)
    cnt = jnp.minimum(t + half_w, seq_len) - jnp.maximum(t - half_w, 0)
    diff = wsum / cnt.astype(F32) - u
    pooled = jnp.dot(diff.astype(BF16), pw_ref[...], preferred_element_type=F32) * ps_ref[...]

    def natural(ref, slab, dil):
        if dil == 1:
            return ref[...]
        for r in range(dil):
            for hf in range(CFG_WIDTH // LANES):
                lo = r * CFG_WIDTH + hf * LANES
                scr[slab + hf, pl.ds(r, tm // dil, stride=dil), :] = ref[:, lo:lo + LANES]
        return jnp.concatenate([scr[slab], scr[slab + 1]], axis=1)

    outs, lses = [], []
    for g, (o_ref, l_ref) in enumerate(((o0_ref, l0_ref), (o1_ref, l1_ref), (o2_ref, l2_ref))):
        dil = ATTN_CONFIGS[g][1]
        outs.append(natural(o_ref, 4 * g, dil))
        lses.append(natural(l_ref, 4 * g + 2, dil))
    mx = jnp.maximum(jnp.maximum(lses[0], lses[1]), lses[2])
    ws = [jnp.exp(l - mx) for l in lses]
    attn = (ws[0] * outs[0] + ws[1] * outs[1] + ws[2] * outs[2]) / (ws[0] + ws[1] + ws[2])

    cat = jnp.concatenate([pooled, attn], axis=-1).astype(BF16)
    mixed = jnp.dot(cat, wout_ref[...], preferred_element_type=F32)
    x1 = x_ref[...] + mod_ref[2:3, :] * mixed
    x1_ref[...] = x1
    h = _rms_mod(x1, g2_ref[...], mod_ref[3:4, :], mod_ref[4:5, :])
    for s in range(TILE_ROWS):
        h_ref[pl.ds(s, tm, stride=TILE_ROWS), :] = h[:, s * LANES:(s + 1) * LANES]
    logits = lax.dot_general(wr_ref[...], h, (((1,), (1,)), ((), ())), preferred_element_type=F32,
                             precision=lax.Precision.HIGHEST)
    z = jnp.exp(logits - jnp.max(logits, axis=0, keepdims=True))
    aff_ref[...] = z / jnp.sum(z, axis=0, keepdims=True)


def _mix(x, mod, g2, u, attn_parts, pool_w_bd, pool_scale, w_out_b, w_router_t):
    B, S, _ = x.shape
    tm = TOKEN_TILE
    halo = POOL_WINDOWS[-1] // 2
    per = tm // halo
    last = S // halo - 1
    nt = S // tm
    tok = lambda w: pl.BlockSpec((None, tm, w), lambda b, i: (b, i, 0))
    const = lambda shape: pl.BlockSpec(shape, lambda b, i: (0,) * len(shape))
    in_specs = [tok(D_MODEL),
                pl.BlockSpec((None, N_MOD, D_MODEL), lambda b, i: (b, 0, 0)),
                const((1, D_MODEL)),
                pl.BlockSpec((None, halo, POOL_WIDTH), lambda b, i: (b, jnp.maximum(i * per - 1, 0), 0)),
                tok(POOL_WIDTH),
                pl.BlockSpec((None, halo, POOL_WIDTH), lambda b, i: (b, jnp.minimum((i + 1) * per, last), 0))]
    for _, dil in ATTN_CONFIGS:
        in_specs += [pl.BlockSpec((None, tm // dil, dil * CFG_WIDTH), lambda b, i: (b, i, 0))] * 2
    in_specs += [const((POOL_WIDTH, POOL_WIDTH)), const((1, POOL_WIDTH)),
                 const((POOL_WIDTH + CFG_WIDTH, D_MODEL)), const((N_EXPERTS, D_MODEL))]
    flat = [a for pair in attn_parts for a in pair]
    return pl.pallas_call(
        functools.partial(_mix_kernel, seq_len=S),
        out_shape=(jax.ShapeDtypeStruct((B, S, D_MODEL), F32),
                   jax.ShapeDtypeStruct((B * S * TILE_ROWS, LANES), F32),
                   jax.ShapeDtypeStruct((N_EXPERTS, B * S), F32)),
        grid=(B, nt),
        in_specs=in_specs,
        out_specs=(tok(D_MODEL),
                   pl.BlockSpec((tm * TILE_ROWS, LANES), lambda b, i: (b * nt + i, 0)),
                   pl.BlockSpec((N_EXPERTS, tm), lambda b, i: (0, b * nt + i))),
        scratch_shapes=[pltpu.VMEM((4 * N_CFG, tm, LANES), F32)],
        compiler_params=_cparams(("parallel", "parallel")),
        name="mix_router",
    )(x, mod, g2, u, u, u, *flat, pool_w_bd, pool_scale, w_out_b, w_router_t)


def _total(x):
    return jnp.sum(jnp.sum(x, axis=0, keepdims=True), axis=1, keepdims=True)


def _flat_prefix_excl(mask, upper, lower_strict):
    xb = jnp.where(mask, 1.0, 0.0).astype(BF16)
    incl = jnp.dot(xb, upper, preferred_element_type=F32)
    above = jnp.sum(jnp.dot(lower_strict, xb, preferred_element_type=F32), axis=1, keepdims=True)
    return incl - xb.astype(F32) + above


def _route_kernel(a_ref, idx_ref, gate_ref, *, capacity):
    a = a_ref[...]
    rows = a.shape[0]
    n = rows * LANES
    bits = pltpu.bitcast(a, I32)
    cap = jnp.float32(capacity)
    thr = jnp.zeros((1, 1), I32)
    for b in range(30, -1, -1):
        cand = thr | (1 << b)
        cnt = _total(jnp.where(bits >= cand, 1.0, 0.0))
        thr = jnp.where(cnt >= cap, cand, thr)
    gt = bits > thr
    eq = bits == thr
    need = cap - _total(jnp.where(gt, 1.0, 0.0))

    li = lax.broadcasted_iota(I32, (LANES, LANES), 0)
    lj = lax.broadcasted_iota(I32, (LANES, LANES), 1)
    upper = jnp.where(li <= lj, 1.0, 0.0).astype(BF16)
    ri = lax.broadcasted_iota(I32, (rows, rows), 0)
    rj = lax.broadcasted_iota(I32, (rows, rows), 1)
    lower_strict = jnp.where(rj < ri, 1.0, 0.0).astype(BF16)

    sel = gt | (eq & (_flat_prefix_excl(eq, upper, lower_strict) < need))
    pos = _flat_prefix_excl(sel, upper, lower_strict).astype(I32)
    lane = lax.broadcasted_iota(I32, (rows, LANES), 1)
    flat = lax.broadcasted_iota(I32, (rows, LANES), 0) * LANES + lane
    d = jnp.where(sel, (flat - pos) | VALID_BIT, 0)
    g = a
    for b in range((n - 1).bit_length()):
        if (1 << b) < LANES:
            s = 1 << b
            keep = lane < LANES - s

            def shift(x, s=s, keep=keep):
                y = pltpu.roll(x, LANES - s, axis=1)
                return jnp.where(keep, y, pltpu.roll(y, rows - 1, axis=0))
        else:
            k = (1 << b) // LANES

            def shift(x, k=k):
                return pltpu.roll(x, rows - k, axis=0)
        d_in = shift(d)
        g_in = shift(g)
        arrive = ((d_in >> b) & 1) == 1
        leave = ((d >> b) & 1) == 1
        g = jnp.where(arrive, g_in, g)
        d = jnp.where(arrive, d_in, jnp.where(leave, 0, d))
    out_rows = capacity // LANES
    idx_ref[...] = (flat + (d & (VALID_BIT - 1)))[:out_rows]
    gate_ref[...] = g[:out_rows]


def _route(aff_t, capacity):
    n = aff_t.shape[1]
    rows = n // LANES
    out_rows = capacity // LANES
    a3 = aff_t.reshape(N_EXPERTS, rows, LANES)
    return pl.pallas_call(
        functools.partial(_route_kernel, capacity=capacity),
        out_shape=(jax.ShapeDtypeStruct((N_EXPERTS, out_rows, LANES), I32),
                   jax.ShapeDtypeStruct((N_EXPERTS, out_rows, LANES), F32)),
        grid=(N_EXPERTS,),
        in_specs=[pl.BlockSpec((None, rows, LANES), lambda e: (e, 0, 0))],
        out_specs=(pl.BlockSpec((None, out_rows, LANES), lambda e: (e, 0, 0)),
                   pl.BlockSpec((None, out_rows, LANES), lambda e: (e, 0, 0))),
        compiler_params=_cparams(("parallel",)),
        name="route",
    )(a3)


def _moe_kernel(idx_hbm, gate_ref, h_hbm, wg_ref, wu_ref, wd_ref, acc_in_hbm, out_hbm,
                idx_smem, x_rows, o_rows, x_buf, acc, sems, *, rows, n_tiles):
    del acc_in_hbm
    t = pl.program_id(0)
    f = pl.program_id(1)
    slot = t % 2
    tr = TILE_ROWS

    def fetch_indices(tile, into):
        cp = pltpu.make_async_copy(idx_hbm.at[tile], idx_smem.at[pl.ds(into * rows, rows)], sems.at[0])
        cp.start()
        cp.wait()

    def issue_rows(hbm, buf, sem, which, lo, count, to_vmem):
        def issue(j, carry):
            r = lo + j
            tok = idx_smem[which * rows + r]
            src = hbm.at[pl.ds(pl.multiple_of(tok * tr, tr), tr)]
            dst = buf.at[pl.ds(pl.multiple_of(r * tr, tr), tr)]
            if not to_vmem:
                src, dst = dst, src
            pltpu.make_async_copy(src, dst, sem).start()
            return carry
        lax.fori_loop(0, count, issue, 0, unroll=8)

    def wait_rows(hbm, buf, sem):
        pltpu.make_async_copy(hbm.at[pl.ds(0, rows * tr)], buf, sem).wait()

    @pl.when(f == 0)
    def _():
        @pl.when(t == 0)
        def _():
            fetch_indices(0, 0)
            issue_rows(h_hbm, x_rows, sems.at[1], 0, 0, rows, True)

        @pl.when(t > 0)
        def _():
            wait_rows(out_hbm, o_rows, sems.at[3])

        issue_rows(out_hbm, o_rows, sems.at[2], slot, 0, rows, True)
        wait_rows(h_hbm, x_rows, sems.at[1])
        for s in range(tr):
            x_buf[:, s * LANES:(s + 1) * LANES] = x_rows[pl.ds(s, rows, stride=tr), :].astype(BF16)
        acc[...] = jnp.zeros_like(acc)

        @pl.when(t + 1 < n_tiles)
        def _():
            fetch_indices(t + 1, 1 - slot)

    @pl.when((f < rows // GATHER_CHUNK) & (t + 1 < n_tiles))
    def _():
        issue_rows(h_hbm, x_rows, sems.at[1], 1 - slot, f * GATHER_CHUNK, GATHER_CHUNK, True)

    x = x_buf[...]
    gate = jnp.dot(x, wg_ref[...].astype(BF16), preferred_element_type=F32)
    up = jnp.dot(x, wu_ref[...].astype(BF16), preferred_element_type=F32)
    hid = (gate * jax.nn.sigmoid(gate) * up).astype(BF16)
    acc[...] += jnp.dot(hid, wd_ref[...].astype(BF16), preferred_element_type=F32)

    @pl.when(f == pl.num_programs(1) - 1)
    def _():
        wait_rows(out_hbm, o_rows, sems.at[2])
        ri = lax.broadcasted_iota(I32, (LANES, LANES), 0)
        ci = lax.broadcasted_iota(I32, (LANES, LANES), 1)
        for c in range(rows // LANES):
            gr = jnp.broadcast_to(gate_ref[:, pl.ds(c * LANES, LANES)], (LANES, LANES))
            gcol = jnp.sum(jnp.where(ri == ci, gr, 0.0), axis=1, keepdims=True)
            for s in range(tr):
                o_rows[pl.ds(c * LANES * tr + s, LANES, stride=tr), :] += (
                    acc[pl.ds(c * LANES, LANES), pl.ds(s * LANES, LANES)] * gcol)
        issue_rows(out_hbm, o_rows, sems.at[3], slot, 0, rows, False)

        @pl.when(t == n_tiles - 1)
        def _():
            wait_rows(out_hbm, o_rows, sems.at[3])


def _moe_ffn(h_tiles, idx, gates, w_gate, w_up, w_down):
    capacity = idx.shape[1] * idx.shape[2]
    rows = min(MOE_ROWS, capacity)
    tiles = capacity // rows
    n_tiles = N_EXPERTS * tiles
    nf = EXPERT_FF // FF_TILE
    assert rows % GATHER_CHUNK == 0 and rows // GATHER_CHUNK <= nf
    idx2 = idx.reshape(n_tiles, rows)
    gates3 = gates.reshape(n_tiles, 1, rows)
    return pl.pallas_call(
        functools.partial(_moe_kernel, rows=rows, n_tiles=n_tiles),
        out_shape=jax.ShapeDtypeStruct(h_tiles.shape, F32),
        grid=(n_tiles, nf),
        in_specs=[pl.BlockSpec(memory_space=pl.ANY),
                  pl.BlockSpec((None, 1, rows), lambda t, f: (t, 0, 0)),
                  pl.BlockSpec(memory_space=pl.ANY),
                  pl.BlockSpec((None, D_MODEL, FF_TILE), lambda t, f: (t // tiles, 0, f)),
                  pl.BlockSpec((None, D_MODEL, FF_TILE), lambda t, f: (t // tiles, 0, f)),
                  pl.BlockSpec((None, FF_TILE, D_MODEL), lambda t, f: (t // tiles, f, 0)),
                  pl.BlockSpec(memory_space=pl.ANY)],
        out_specs=pl.BlockSpec(memory_space=pl.ANY),
        scratch_shapes=[pltpu.SMEM((2 * rows,), I32),
                        pltpu.VMEM((rows * TILE_ROWS, LANES), F32),
                        pltpu.VMEM((rows * TILE_ROWS, LANES), F32),
                        pltpu.VMEM((rows, D_MODEL), BF16),
                        pltpu.VMEM((rows, D_MODEL), F32),
                        pltpu.SemaphoreType.DMA((4,))],
        input_output_aliases={6: 0},
        compiler_params=_cparams(("arbitrary", "arbitrary")),
        name="moe_ffn",
    )(idx2, gates3, h_tiles, w_gate, w_up, w_down, jnp.zeros(h_tiles.shape, F32))


def _final_kernel(x_ref, moe_ref, mod_ref, g_ref, o_ref):
    tm = TOKEN_TILE
    moe = jnp.concatenate([moe_ref[pl.ds(s, tm, stride=TILE_ROWS), :] for s in range(TILE_ROWS)], axis=1)
    x = x_ref[...] + mod_ref[5:6, :] * moe
    o_ref[...] = (x * lax.rsqrt(jnp.mean(x * x, axis=-1, keepdims=True) + RMS_EPS)) * g_ref[...]


def _final(x1, moe_tiles, mod, gf):
    B, S, _ = x1.shape
    tm = TOKEN_TILE
    nt = S // tm
    tok = pl.BlockSpec((None, tm, D_MODEL), lambda b, i: (b, i, 0))
    return pl.pallas_call(
        _final_kernel,
        out_shape=jax.ShapeDtypeStruct((B, S, D_MODEL), F32),
        grid=(B, nt),
        in_specs=[tok,
                  pl.BlockSpec((tm * TILE_ROWS, LANES), lambda b, i: (b * nt + i, 0)),
                  pl.BlockSpec((None, N_MOD, D_MODEL), lambda b, i: (b, 0, 0)),
                  pl.BlockSpec((1, D_MODEL), lambda b, i: (0, 0))],
        out_specs=tok,
        compiler_params=_cparams(("parallel", "parallel")),
        name="final_norm",
    )(x1, moe_tiles, mod, gf)


def _trunk_group(x, c, p):
    B, S, _ = x.shape
    n = B * S
    mod = _ada(c, p["w_ada"], p["b_ada"])
    cos, sin = _rope_tables(S)
    u, *qkv = _in_proj(x, mod, p["g1"], p["w_in"], cos, sin)
    parts = [_attention(*qkv[3 * g:3 * g + 3], p["band_bias"]) for g in range(N_CFG)]
    x1, h_tiles, aff_t = _mix(x, mod, p["g2"], u, parts, p["pool_w"], p["pool_scale"], p["w_out"], p["w_router_t"])
    idx, gates = _route(aff_t, CAPACITY_FACTOR * n // N_EXPERTS)
    moe_tiles = _moe_ffn(h_tiles, idx, gates, p["w_gate"], p["w_up"], p["w_down"])
    return _final(x1, moe_tiles, mod, p["gf"])


def _layer_params(norm1_g, norm2_g, normf_g, w_ada, b_ada, w_in, pool_w, pool_scale, w_out, w_router, w_gate,
                  w_up, w_down):
    assert norm1_g.shape[0] == 1, "one encoder layer"
    pool_bd = jax.scipy.linalg.block_diag(*[pool_w[0, g] for g in range(len(POOL_WINDOWS))])
    return {
        "g1": norm1_g[0].reshape(1, D_MODEL), "g2": norm2_g[0].reshape(1, D_MODEL),
        "gf": normf_g.reshape(1, D_MODEL),
        "w_ada": w_ada[0], "b_ada": b_ada[0],
        "w_in": jnp.take(w_in[0], _qk_column_order(), axis=1).astype(BF16),
        "band_bias": jnp.asarray(_band_bias()),
        "pool_w": pool_bd.astype(BF16), "pool_scale": pool_scale[0].reshape(1, POOL_WIDTH),
        "w_out": w_out[0].astype(BF16), "w_router_t": w_router[0].T,
        "w_gate": w_gate[0], "w_up": w_up[0], "w_down": w_down[0],
    }


def kernel(x_prompt, x_sample, c_prompt, c_sample, norm1_g, norm2_g, normf_g, w_ada, b_ada, w_in, pool_w,
           pool_scale, w_out, w_router, w_gate, w_up, w_down):
    p = _layer_params(norm1_g, norm2_g, normf_g, w_ada, b_ada, w_in, pool_w, pool_scale, w_out, w_router,
                      w_gate, w_up, w_down)
    return (_trunk_group(x_prompt, c_prompt, p), _trunk_group(x_sample, c_sample, p))
```

```python
import functools
import math

import numpy as np
import jax
import jax.numpy as jnp
from jax import lax
from jax.experimental import pallas as pl
from jax.experimental.pallas import tpu as pltpu

F32 = jnp.float32
BF16 = jnp.bfloat16
I32 = jnp.int32

D_MODEL = 1024
POOL_WINDOWS = (2, 4, 8, 16)
POOL_GROUP = 64
POOL_WIDTH = POOL_GROUP * len(POOL_WINDOWS)
HEAD_DIM = 64
HALF = HEAD_DIM // 2
ATTN_CONFIGS = ((128, 1), (512, 4), (2048, 16))
HEADS_PER_CONFIG = 4
CFG_WIDTH = HEADS_PER_CONFIG * HEAD_DIM
N_CFG = len(ATTN_CONFIGS)
ATTN_WIDTH = N_CFG * CFG_WIDTH
IN_WIDTH = POOL_WIDTH + 3 * ATTN_WIDTH
RADIUS = 64
ROPE_THETA = 10000.0
N_EXPERTS = 16
CAPACITY_FACTOR = 2
EXPERT_FF = 2816
N_MOD = 6
RMS_EPS = 1e-6
NEG_INF = -1e30

LANES = 128
SUBLANES = 8
TILE_ROWS = D_MODEL // LANES
VMEM_LIMIT = 56 * 1024 * 1024

TOKEN_TILE = 512
QUERY_BLOCK = 128
ATTN_CHUNK = 512
FF_TILE = 256
MOE_ROWS = 2048
COPY_STEPS = 8
VALID_BIT = 1 << 20

assert all(w // (2 * d) == RADIUS for w, d in ATTN_CONFIGS)
assert TILE_ROWS == SUBLANES


def _cparams(sem):
    return pltpu.CompilerParams(dimension_semantics=sem, vmem_limit_bytes=VMEM_LIMIT)


def _ada_kernel(c_ref, w_ref, b_ref, o_ref):
    c = c_ref[...]
    a = c * jax.nn.sigmoid(c)
    o_ref[...] = jnp.dot(a, w_ref[...], preferred_element_type=F32,
                         precision=lax.Precision.HIGHEST) + b_ref[...]


def _ada(c, w_ada, b_ada):
    B = c.shape[0]
    rows = -(-B // SUBLANES) * SUBLANES
    cp = jnp.pad(c, ((0, rows - B), (0, 0)))
    width = N_MOD * D_MODEL
    tile = 768
    out = pl.pallas_call(
        _ada_kernel,
        out_shape=jax.ShapeDtypeStruct((rows, width), F32),
        grid=(width // tile,),
        in_specs=[pl.BlockSpec((rows, D_MODEL), lambda j: (0, 0)),
                  pl.BlockSpec((D_MODEL, tile), lambda j: (0, j)),
                  pl.BlockSpec((1, tile), lambda j: (0, j))],
        out_specs=pl.BlockSpec((rows, tile), lambda j: (0, j)),
        compiler_params=_cparams(("arbitrary",)),
        name="ada",
    )(cp, w_ada, b_ada.reshape(1, width))
    return out[:B].reshape(B, N_MOD, D_MODEL)


def _rms_mod(x, g, shift, scale):
    y = x * lax.rsqrt(jnp.mean(x * x, axis=-1, keepdims=True) + RMS_EPS)
    return (y * g) * (1.0 + scale) + shift


def _inproj_kernel(x_ref, mod_ref, g_ref, w_ref, cos_ref, sin_ref, u_ref, *rest):
    qkv = (rest[0:3], rest[3:6], rest[6:9])
    scr = rest[9]
    tm = TOKEN_TILE
    h = _rms_mod(x_ref[...], g_ref[...], mod_ref[0:1, :], mod_ref[1:2, :])
    proj = jnp.dot(h.astype(BF16), w_ref[...], preferred_element_type=F32)
    u_ref[...] = proj[:, :POOL_WIDTH]
    cos = cos_ref[...]
    sin = sin_ref[...]
    slab = 0
    for g, (_, dil) in enumerate(ATTN_CONFIGS):
        pieces = []
        for part, scl in ((0, 1.0 / math.sqrt(HEAD_DIM)), (1, 1.0)):
            off = POOL_WIDTH + part * ATTN_WIDTH + g * CFG_WIDTH
            x1 = proj[:, off:off + LANES]
            x2 = proj[:, off + LANES:off + CFG_WIDTH]
            pieces.append((qkv[g][part], 0, (x1 * cos - x2 * sin) * scl))
            pieces.append((qkv[g][part], LANES, (x2 * cos + x1 * sin) * scl))
        voff = POOL_WIDTH + 2 * ATTN_WIDTH + g * CFG_WIDTH
        pieces.append((qkv[g][2], 0, proj[:, voff:voff + LANES]))
        pieces.append((qkv[g][2], LANES, proj[:, voff + LANES:voff + CFG_WIDTH]))
        for ref, lo, val in pieces:
            if dil == 1:
                ref[0, :, lo:lo + LANES] = val.astype(BF16)
            else:
                scr[slab] = val
                for r in range(dil):
                    ref[r, :, lo:lo + LANES] = scr[slab, pl.ds(r, tm // dil, stride=dil), :].astype(BF16)
                slab += 1


def _qk_column_order():
    cols = list(range(POOL_WIDTH))
    for part in range(2):
        base = POOL_WIDTH + part * ATTN_WIDTH
        for g in range(N_CFG):
            for half in range(2):
                for j in range(HEADS_PER_CONFIG):
                    start = base + (g * HEADS_PER_CONFIG + j) * HEAD_DIM + half * HALF
                    cols.extend(range(start, start + HALF))
    cols.extend(range(POOL_WIDTH + 2 * ATTN_WIDTH, IN_WIDTH))
    return np.asarray(cols, dtype=np.int32)


def _rope_tables(S):
    inv_freq = ROPE_THETA ** (-jnp.arange(HALF, dtype=F32) / HALF)
    ang = jnp.arange(S, dtype=F32)[:, None] * inv_freq[None, :]
    reps = LANES // HALF
    return jnp.tile(jnp.cos(ang), (1, reps)), jnp.tile(jnp.sin(ang), (1, reps))


def _in_proj(x, mod, g1, w_in_b, cos, sin):
    B, S, _ = x.shape
    tm = TOKEN_TILE
    tok = lambda w: pl.BlockSpec((None, tm, w), lambda b, i: (b, i, 0))
    out_shape = [jax.ShapeDtypeStruct((B, S, POOL_WIDTH), F32)]
    out_specs = [tok(POOL_WIDTH)]
    slabs = 0
    for _, dil in ATTN_CONFIGS:
        assert tm % (dil * 2 * SUBLANES) == 0 and S % (dil * ATTN_CHUNK) == 0
        slabs += 0 if dil == 1 else 3 * CFG_WIDTH // LANES
        for _ in range(3):
            out_shape.append(jax.ShapeDtypeStruct((B, dil, S // dil, CFG_WIDTH), BF16))
            out_specs.append(pl.BlockSpec((None, dil, tm // dil, CFG_WIDTH), lambda b, i: (b, 0, i, 0)))
    return pl.pallas_call(
        _inproj_kernel,
        out_shape=tuple(out_shape),
        grid=(B, S // tm),
        in_specs=[tok(D_MODEL),
                  pl.BlockSpec((None, N_MOD, D_MODEL), lambda b, i: (b, 0, 0)),
                  pl.BlockSpec((1, D_MODEL), lambda b, i: (0, 0)),
                  pl.BlockSpec((D_MODEL, IN_WIDTH), lambda b, i: (0, 0)),
                  pl.BlockSpec((tm, LANES), lambda b, i: (i, 0)),
                  pl.BlockSpec((tm, LANES), lambda b, i: (i, 0))],
        out_specs=tuple(out_specs),
        scratch_shapes=[pltpu.VMEM((slabs, tm, LANES), F32)],
        compiler_params=_cparams(("parallel", "parallel")),
        name="in_proj",
    )(x, mod, g1, w_in_b, cos, sin)


def _band_bias():
    rows = HEADS_PER_CONFIG * QUERY_BLOCK
    nk = QUERY_BLOCK + 2 * RADIUS
    qi = np.arange(rows)[:, None] % QUERY_BLOCK
    kj = np.arange(nk)[None, :]
    band = np.abs(kj - RADIUS - qi) <= RADIUS
    tables = [band, band & (kj >= RADIUS), band & (kj < nk - RADIUS)]
    return np.stack([np.where(t, 0.0, NEG_INF) for t in tables]).astype(np.float32)


def _attn_kernel(q_ref, kp_ref, kc_ref, kn_ref, vp_ref, vc_ref, vn_ref, bias_ref, o_ref, lse_ref):
    i = pl.program_id(2)
    last = pl.num_programs(2) - 1
    qb = QUERY_BLOCK
    nsub = ATTN_CHUNK // qb
    nk = qb + 2 * RADIUS
    rows = HEADS_PER_CONFIG * qb
    lane = lax.broadcasted_iota(I32, (qb, CFG_WIDTH), 1)
    q_head = (lane % LANES) // HALF
    v_head = lane // HEAD_DIM
    k = jnp.concatenate([kp_ref[...], kc_ref[...], kn_ref[...]], axis=0)
    v = jnp.concatenate([vp_ref[...], vc_ref[...], vn_ref[...]], axis=0)
    for j in range(nsub):
        q = q_ref[j * qb:(j + 1) * qb, :]
        zero = jnp.zeros_like(q)
        q_stack = jnp.concatenate([jnp.where(q_head == hd, q, zero) for hd in range(HEADS_PER_CONFIG)], axis=0)
        s = lax.dot_general(q_stack, k[j * qb:j * qb + nk], (((1,), (1,)), ((), ())),
                            preferred_element_type=F32)
        which = 0
        if j == 0:
            which = jnp.where(i == 0, 1, 0)
        if j == nsub - 1:
            which = jnp.where(i == last, 2, 0)
        s = s + bias_ref[which]
        m = jnp.max(s, axis=-1, keepdims=True)
        e = jnp.exp(s - m)
        l = jnp.sum(e, axis=-1, keepdims=True)
        r = jnp.dot(e.astype(BF16), v[j * qb:j * qb + nk], preferred_element_type=F32) / l
        lse = jnp.broadcast_to(m + jnp.log(l), (rows, CFG_WIDTH))
        o = jnp.zeros((qb, CFG_WIDTH), F32)
        lse_o = jnp.zeros((qb, CFG_WIDTH), F32)
        for hd in range(HEADS_PER_CONFIG):
            o = jnp.where(v_head == hd, r[hd * qb:(hd + 1) * qb], o)
            lse_o = jnp.where(v_head == hd, lse[hd * qb:(hd + 1) * qb], lse_o)
        o_ref[j * qb:(j + 1) * qb, :] = o
        lse_ref[j * qb:(j + 1) * qb, :] = lse_o


def _attention(q, k, v, bias):
    B, dil, L, _ = q.shape
    ch = ATTN_CHUNK
    assert ch // QUERY_BLOCK >= 2 and L % ch == 0
    per = ch // RADIUS
    last = L // RADIUS - 1
    cur = pl.BlockSpec((None, None, ch, CFG_WIDTH), lambda b, r, i: (b, r, i, 0))
    prev = pl.BlockSpec((None, None, RADIUS, CFG_WIDTH), lambda b, r, i: (b, r, jnp.maximum(i * per - 1, 0), 0))
    nxt = pl.BlockSpec((None, None, RADIUS, CFG_WIDTH), lambda b, r, i: (b, r, jnp.minimum((i + 1) * per, last), 0))
    out = pl.BlockSpec((None, ch, CFG_WIDTH), lambda b, r, i: (b, i, r))
    return pl.pallas_call(
        _attn_kernel,
        out_shape=(jax.ShapeDtypeStruct((B, L, dil * CFG_WIDTH), F32),
                   jax.ShapeDtypeStruct((B, L, dil * CFG_WIDTH), F32)),
        grid=(B, dil, L // ch),
        in_specs=[cur, prev, cur, nxt, prev, cur, nxt,
                  pl.BlockSpec(bias.shape, lambda b, r, i: (0, 0, 0))],
        out_specs=(out, out),
        compiler_params=_cparams(("parallel", "parallel", "parallel")),
        name="banded_attention",
    )(q, k, k, k, v, v, v, bias)


def _mix_kernel(x_ref, mod_ref, g2_ref, up_ref, uc_ref, un_ref, o0_ref, l0_ref, o1_ref, l1_ref, o2_ref, l2_ref,
                pw_ref, ps_ref, wout_ref, wr_ref, *rest, seq_len):
    x1_ref, h_ref, aff_ref, scr = rest[-4:]
    i = pl.program_id(1)
    tm = TOKEN_TILE
    u = uc_ref[...]
    up = jnp.where(i > 0, up_ref[...], 0.0)
    un = jnp.where(i < pl.num_programs(1) - 1, un_ref[...], 0.0)
    ext = jnp.concatenate([up, u, un], axis=0)
    p2 = ext[:-1] + ext[1:]
    q4 = p2[:-2] + p2[2:]
    q8 = q4[:-4] + q4[4:]
    q16 = q8[:-8] + q8[8:]
    lane = lax.broadcasted_iota(I32, (tm, POOL_WIDTH), 1)
    grp = lane // POOL_GROUP
    wsum = jnp.where(grp == 0, p2[7:7 + tm],
                     jnp.where(grp == 1, q4[6:6 + tm], jnp.where(grp == 2, q8[4:4 + tm], q16[0:tm])))
    half_w = jnp.left_shift(1, grp)
    t = i * tm + lax.broadcasted_iota(I32, (tm, POOL_WIDTH), 0)
    cnt = jnp.minimum(t + half_w, seq_len) - jnp.maximum(t - half_w, 0)
    diff = wsum / cnt.astype(F32) - u
    pooled = jnp.dot(diff.astype(BF16), pw_ref[...], preferred_element_type=F32) * ps_ref[...]

    def natural(ref, slab, dil):
        if dil == 1:
            return ref[...]
        for r in range(dil):
            for hf in range(CFG_WIDTH // LANES):
                lo = r * CFG_WIDTH + hf * LANES
                scr[slab + hf, pl.ds(r, tm // dil, stride=dil), :] = ref[:, lo:lo + LANES]
        return jnp.concatenate([scr[slab], scr[slab + 1]], axis=1)

    outs, lses = [], []
    for g, (o_ref, l_ref) in enumerate(((o0_ref, l0_ref), (o1_ref, l1_ref), (o2_ref, l2_ref))):
        dil = ATTN_CONFIGS[g][1]
        outs.append(natural(o_ref, 4 * g, dil))
        lses.append(natural(l_ref, 4 * g + 2, dil))
    mx = jnp.maximum(jnp.maximum(lses[0], lses[1]), lses[2])
    ws = [jnp.exp(l - mx) for l in lses]
    attn = (ws[0] * outs[0] + ws[1] * outs[1] + ws[2] * outs[2]) / (ws[0] + ws[1] + ws[2])

    cat = jnp.concatenate([pooled, attn], axis=-1).astype(BF16)
    mixed = jnp.dot(cat, wout_ref[...], preferred_element_type=F32)
    x1 = x_ref[...] + mod_ref[2:3, :] * mixed
    x1_ref[...] = x1
    h = _rms_mod(x1, g2_ref[...], mod_ref[3:4, :], mod_ref[4:5, :])
    for s in range(TILE_ROWS):
        h_ref[pl.ds(s, tm, stride=TILE_ROWS), :] = h[:, s * LANES:(s + 1) * LANES]
    logits = lax.dot_general(wr_ref[...], h, (((1,), (1,)), ((), ())), preferred_element_type=F32,
                             precision=lax.Precision.HIGHEST)
    z = jnp.exp(logits - jnp.max(logits, axis=0, keepdims=True))
    aff_ref[...] = z / jnp.sum(z, axis=0, keepdims=True)


def _mix(x, mod, g2, u, attn_parts, pool_w_bd, pool_scale, w_out_b, w_router_t, h_before):
    B, S, _ = x.shape
    tm = TOKEN_TILE
    halo = POOL_WINDOWS[-1] // 2
    per = tm // halo
    last = S // halo - 1
    nt = S // tm
    tok = lambda w: pl.BlockSpec((None, tm, w), lambda b, i: (b, i, 0))
    const = lambda shape: pl.BlockSpec(shape, lambda b, i: (0,) * len(shape))
    in_specs = [tok(D_MODEL),
                pl.BlockSpec((None, N_MOD, D_MODEL), lambda b, i: (b, 0, 0)),
                const((1, D_MODEL)),
                pl.BlockSpec((None, halo, POOL_WIDTH), lambda b, i: (b, jnp.maximum(i * per - 1, 0), 0)),
                tok(POOL_WIDTH),
                pl.BlockSpec((None, halo, POOL_WIDTH), lambda b, i: (b, jnp.minimum((i + 1) * per, last), 0))]
    for _, dil in ATTN_CONFIGS:
        in_specs += [pl.BlockSpec((None, tm // dil, dil * CFG_WIDTH), lambda b, i: (b, i, 0))] * 2
    in_specs += [const((POOL_WIDTH, POOL_WIDTH)), const((1, POOL_WIDTH)),
                 const((POOL_WIDTH + CFG_WIDTH, D_MODEL)), const((N_EXPERTS, D_MODEL))]
    args = [x, mod, g2, u, u, u] + [a for pair in attn_parts for a in pair]
    args += [pool_w_bd, pool_scale, w_out_b, w_router_t]
    h_rows, first_tile, aliases = B * S * TILE_ROWS, 0, {}
    if h_before is not None:
        first_tile = h_before.shape[0] // (tm * TILE_ROWS)
        h_rows += h_before.shape[0]
        aliases = {len(args): 1}
        in_specs.append(pl.BlockSpec(memory_space=pl.ANY))
        args.append(jnp.pad(h_before, ((0, B * S * TILE_ROWS), (0, 0))))
    return pl.pallas_call(
        functools.partial(_mix_kernel, seq_len=S),
        out_shape=(jax.ShapeDtypeStruct((B, S, D_MODEL), F32),
                   jax.ShapeDtypeStruct((h_rows, LANES), F32),
                   jax.ShapeDtypeStruct((N_EXPERTS, B * S), F32)),
        grid=(B, nt),
        in_specs=in_specs,
        out_specs=(tok(D_MODEL),
                   pl.BlockSpec((tm * TILE_ROWS, LANES), lambda b, i: (first_tile + b * nt + i, 0)),
                   pl.BlockSpec((N_EXPERTS, tm), lambda b, i: (0, b * nt + i))),
        scratch_shapes=[pltpu.VMEM((4 * N_CFG, tm, LANES), F32)],
        input_output_aliases=aliases,
        compiler_params=_cparams(("parallel", "parallel")),
        name="mix_router",
    )(*args)


def _total(x):
    return jnp.sum(jnp.sum(x, axis=0, keepdims=True), axis=1, keepdims=True)


def _flat_prefix_excl(mask, upper, lower_strict):
    xb = jnp.where(mask, 1.0, 0.0).astype(BF16)
    incl = jnp.dot(xb, upper, preferred_element_type=F32)
    above = jnp.sum(jnp.dot(lower_strict, xb, preferred_element_type=F32), axis=1, keepdims=True)
    return incl - xb.astype(F32) + above


def _route_kernel(a_ref, idx_ref, gate_ref, *, capacity):
    a = a_ref[...]
    rows = a.shape[0]
    n = rows * LANES
    cap = jnp.float32(capacity)
    thr = jnp.zeros((1, 1), I32)
    for b in range(30, -1, -1):
        cand = thr | (1 << b)
        cnt = _total(jnp.where(a >= pltpu.bitcast(cand, F32), 1.0, 0.0))
        thr = jnp.where(cnt >= cap, cand, thr)
    gt = a >= pltpu.bitcast(thr + 1, F32)
    eq = (a >= pltpu.bitcast(thr, F32)) & jnp.logical_not(gt)
    need = cap - _total(jnp.where(gt, 1.0, 0.0))

    li = lax.broadcasted_iota(I32, (LANES, LANES), 0)
    lj = lax.broadcasted_iota(I32, (LANES, LANES), 1)
    upper = jnp.where(li <= lj, 1.0, 0.0).astype(BF16)
    ri = lax.broadcasted_iota(I32, (rows, rows), 0)
    rj = lax.broadcasted_iota(I32, (rows, rows), 1)
    lower_strict = jnp.where(rj < ri, 1.0, 0.0).astype(BF16)

    sel = gt | (eq & (_flat_prefix_excl(eq, upper, lower_strict) < need))
    pos = _flat_prefix_excl(sel, upper, lower_strict).astype(I32)
    lane = lax.broadcasted_iota(I32, (rows, LANES), 1)
    flat = lax.broadcasted_iota(I32, (rows, LANES), 0) * LANES + lane
    d = jnp.where(sel, (flat - pos) | VALID_BIT, 0)
    g = a
    for b in range((n - 1).bit_length()):
        if (1 << b) < LANES:
            s = 1 << b
            keep = lane < LANES - s

            def shift(x, s=s, keep=keep):
                y = pltpu.roll(x, LANES - s, axis=1)
                return jnp.where(keep, y, pltpu.roll(y, rows - 1, axis=0))
        else:
            k = (1 << b) // LANES

            def shift(x, k=k):
                return pltpu.roll(x, rows - k, axis=0)
        d_in = shift(d)
        g_in = shift(g)
        arrive = ((d_in >> b) & 1) == 1
        leave = ((d >> b) & 1) == 1
        g = jnp.where(arrive, g_in, g)
        d = jnp.where(arrive, d_in, jnp.where(leave, 0, d))
    out_rows = capacity // LANES
    idx_ref[...] = (flat + (d & (VALID_BIT - 1)))[:out_rows]
    gate_ref[...] = g[:out_rows]


def _route(aff_t, capacity):
    n = aff_t.shape[1]
    rows = n // LANES
    out_rows = capacity // LANES
    a3 = aff_t.reshape(N_EXPERTS, rows, LANES)
    return pl.pallas_call(
        functools.partial(_route_kernel, capacity=capacity),
        out_shape=(jax.ShapeDtypeStruct((N_EXPERTS, out_rows, LANES), I32),
                   jax.ShapeDtypeStruct((N_EXPERTS, out_rows, LANES), F32)),
        grid=(N_EXPERTS,),
        in_specs=[pl.BlockSpec((None, rows, LANES), lambda e: (e, 0, 0))],
        out_specs=(pl.BlockSpec((None, out_rows, LANES), lambda e: (e, 0, 0)),
                   pl.BlockSpec((None, out_rows, LANES), lambda e: (e, 0, 0))),
        compiler_params=_cparams(("parallel",)),
        name="route",
    )(a3)


def _moe_kernel(idx_hbm, gate_ref, h_hbm, wg_ref, wu_ref, wd_ref, acc_in_hbm, out_hbm,
                idx_smem, x_rows, o_rows, x_buf, acc, sems, *, rows, n_tiles, n_steps):
    del acc_in_hbm
    t = pl.program_id(0)
    f = pl.program_id(1)
    tr = TILE_ROWS
    per = rows // COPY_STEPS
    idx_cur, idx_nxt, idx_prv = (t % 3) * rows, ((t + 1) % 3) * rows, ((t + 2) % 3) * rows
    o_cur = pl.multiple_of((t % 2) * (rows * tr), tr)
    o_prv = pl.multiple_of(((t + 1) % 2) * (rows * tr), tr)

    def fetch_indices(tile, into):
        cp = pltpu.make_async_copy(idx_hbm.at[tile], idx_smem.at[pl.ds(into, rows)], sems.at[0])
        cp.start()
        cp.wait()

    def row_copy(r, idx_base, hbm, buf, buf_base, sem, to_vmem):
        tok = idx_smem[idx_base + r]
        in_hbm = hbm.at[pl.ds(pl.multiple_of(tok * tr, tr), tr)]
        in_vmem = buf.at[pl.ds(pl.multiple_of(buf_base + r * tr, tr), tr)]
        src, dst = (in_hbm, in_vmem) if to_vmem else (in_vmem, in_hbm)
        return pltpu.make_async_copy(src, dst, sem)

    def issue_all(idx_base, hbm, buf, buf_base, sem, to_vmem):
        def issue(r, carry):
            row_copy(r, idx_base, hbm, buf, buf_base, sem, to_vmem).start()
            return carry
        lax.fori_loop(0, rows, issue, 0, unroll=8)

    def issue_slices(streams):
        first = jnp.minimum(f, COPY_STEPS - 1) * per
        in_window = f < COPY_STEPS
        conds = [live & in_window for live, *_ in streams]
        for j in range(per):
            for cond, (_, *args) in zip(conds, streams):
                @pl.when(cond)
                def _(args=args, j=j):
                    row_copy(first + j, *args).start()

    def wait_rows(hbm, buf, buf_base, sem):
        pltpu.make_async_copy(hbm.at[pl.ds(0, rows * tr)], buf.at[pl.ds(buf_base, rows * tr)], sem).wait()

    @pl.when(f == 0)
    def _():
        @pl.when(t == 0)
        def _():
            fetch_indices(0, 0)
            issue_all(0, h_hbm, x_rows, 0, sems.at[1], True)

        @pl.when(t >= 2)
        def _():
            wait_rows(out_hbm, o_rows, o_cur, sems.at[3])

        wait_rows(out_hbm, x_rows, 0, sems.at[1])
        for s in range(tr):
            x_buf[:, s * LANES:(s + 1) * LANES] = x_rows[pl.ds(s, rows, stride=tr), :].astype(BF16)
        acc[...] = jnp.zeros_like(acc)

        @pl.when(t + 1 < n_tiles)
        def _():
            fetch_indices(t + 1, idx_nxt)

    issue_slices([(t + 1 < n_tiles, idx_nxt, h_hbm, x_rows, 0, sems.at[1], True),
                  (t >= 0, idx_cur, out_hbm, o_rows, o_cur, sems.at[2], True),
                  (t > 0, idx_prv, out_hbm, o_rows, o_prv, sems.at[3], False)])

    x = x_buf[...]
    gate = jnp.dot(x, wg_ref[...].astype(BF16), preferred_element_type=F32)
    up = jnp.dot(x, wu_ref[...].astype(BF16), preferred_element_type=F32)
    hid = (gate * jax.nn.sigmoid(gate) * up).astype(BF16)
    acc[...] += jnp.dot(hid, wd_ref[...].astype(BF16), preferred_element_type=F32)

    @pl.when(f == n_steps - 1)
    def _():
        wait_rows(out_hbm, o_rows, o_cur, sems.at[2])
        ri = lax.broadcasted_iota(I32, (LANES, LANES), 0)
        ci = lax.broadcasted_iota(I32, (LANES, LANES), 1)
        for c in range(rows // LANES):
            gr = jnp.broadcast_to(gate_ref[:, pl.ds(c * LANES, LANES)], (LANES, LANES))
            gcol = jnp.sum(jnp.where(ri == ci, gr, 0.0), axis=1, keepdims=True)
            for s in range(tr):
                o_rows[pl.ds(o_cur + (c * LANES * tr + s), LANES, stride=tr), :] += (
                    acc[pl.ds(c * LANES, LANES), pl.ds(s * LANES, LANES)] * gcol)

        @pl.when(t == n_tiles - 1)
        def _():
            @pl.when(t > 0)
            def _():
                wait_rows(out_hbm, o_rows, o_prv, sems.at[3])
            issue_all(idx_cur, out_hbm, o_rows, o_cur, sems.at[3], False)
            wait_rows(out_hbm, o_rows, o_cur, sems.at[3])


def _moe_ffn(h_all, routed, w_gate, w_up, w_down):
    assert len(routed) >= 2, "the tile order needs two request groups to separate experts"
    rows = min([MOE_ROWS] + [idx.shape[1] * idx.shape[2] for idx, _, _ in routed])
    idx_parts, gate_parts = [], []
    for idx, gates, first_token in routed:
        capacity = idx.shape[1] * idx.shape[2]
        assert capacity % rows == 0
        idx_parts.append(idx.reshape(N_EXPERTS, capacity // rows, rows) + first_token)
        gate_parts.append(gates.reshape(N_EXPERTS, capacity // rows, rows))
    per_expert = sum(part.shape[1] for part in idx_parts)
    any_space = pl.BlockSpec(memory_space=pl.ANY)
    n_tiles = N_EXPERTS * per_expert
    n_steps = EXPERT_FF // FF_TILE
    assert rows % COPY_STEPS == 0 and COPY_STEPS < n_steps
    idx2 = jnp.concatenate(idx_parts, axis=1).reshape(n_tiles, rows)
    gates3 = jnp.concatenate(gate_parts, axis=1).reshape(n_tiles, 1, rows)
    return pl.pallas_call(
        functools.partial(_moe_kernel, rows=rows, n_tiles=n_tiles, n_steps=n_steps),
        out_shape=jax.ShapeDtypeStruct(h_all.shape, F32),
        grid=(n_tiles, n_steps),
        in_specs=[any_space, pl.BlockSpec((None, 1, rows), lambda t, f: (t, 0, 0)), any_space,
                  pl.BlockSpec((None, D_MODEL, FF_TILE), lambda t, f: (t // per_expert, 0, f)),
                  pl.BlockSpec((None, D_MODEL, FF_TILE), lambda t, f: (t // per_expert, 0, f)),
                  pl.BlockSpec((None, FF_TILE, D_MODEL), lambda t, f: (t // per_expert, f, 0)),
                  any_space],
        out_specs=any_space,
        scratch_shapes=[pltpu.SMEM((3 * rows,), I32),
                        pltpu.VMEM((rows * TILE_ROWS, LANES), F32),
                        pltpu.VMEM((2 * rows * TILE_ROWS, LANES), F32),
                        pltpu.VMEM((rows, D_MODEL), BF16),
                        pltpu.VMEM((rows, D_MODEL), F32),
                        pltpu.SemaphoreType.DMA((4,))],
        input_output_aliases={6: 0},
        compiler_params=_cparams(("arbitrary", "arbitrary")),
        name="moe_ffn",
    )(idx2, gates3, h_all, w_gate, w_up, w_down, jnp.zeros(h_all.shape, F32))


def _final_kernel(x_ref, moe_ref, mod_ref, g_ref, o_ref):
    tm = TOKEN_TILE
    moe = jnp.concatenate([moe_ref[pl.ds(s, tm, stride=TILE_ROWS), :] for s in range(TILE_ROWS)], axis=1)
    x = x_ref[...] + mod_ref[5:6, :] * moe
    o_ref[...] = (x * lax.rsqrt(jnp.mean(x * x, axis=-1, keepdims=True) + RMS_EPS)) * g_ref[...]


def _final(x1, moe_tiles, mod, gf, tile_offset):
    B, S, _ = x1.shape
    tm = TOKEN_TILE
    nt = S // tm
    tok = pl.BlockSpec((None, tm, D_MODEL), lambda b, i: (b, i, 0))
    return pl.pallas_call(
        _final_kernel,
        out_shape=jax.ShapeDtypeStruct((B, S, D_MODEL), F32),
        grid=(B, nt),
        in_specs=[tok,
                  pl.BlockSpec((tm * TILE_ROWS, LANES), lambda b, i: (tile_offset + b * nt + i, 0)),
                  pl.BlockSpec((None, N_MOD, D_MODEL), lambda b, i: (b, 0, 0)),
                  pl.BlockSpec((1, D_MODEL), lambda b, i: (0, 0))],
        out_specs=tok,
        compiler_params=_cparams(("parallel", "parallel")),
        name="final_norm",
    )(x1, moe_tiles, mod, gf)


def _layer(groups, p):
    h_all, offset, fronts, routed = None, 0, [], []
    for x, c in groups:
        B, S, _ = x.shape
        n = B * S
        mod = _ada(c, p["w_ada"], p["b_ada"])
        cos, sin = _rope_tables(S)
        u, *qkv = _in_proj(x, mod, p["g1"], p["w_in"], cos, sin)
        parts = [_attention(*qkv[3 * g:3 * g + 3], p["band_bias"]) for g in range(N_CFG)]
        x1, h_all, aff_t = _mix(x, mod, p["g2"], u, parts, p["pool_w"], p["pool_scale"], p["w_out"],
                                p["w_router_t"], h_all)
        idx, gates = _route(aff_t, CAPACITY_FACTOR * n // N_EXPERTS)
        fronts.append((x1, mod, offset // TOKEN_TILE))
        routed.append((idx, gates, offset))
        offset += n
    moe_tiles = _moe_ffn(h_all, routed, p["w_gate"], p["w_up"], p["w_down"])
    return tuple(_final(x1, moe_tiles, mod, p["gf"], tile_offset) for x1, mod, tile_offset in fronts)


def _layer_params(norm1_g, norm2_g, normf_g, w_ada, b_ada, w_in, pool_w, pool_scale, w_out, w_router, w_gate,
                  w_up, w_down):
    assert norm1_g.shape[0] == 1, "one encoder layer"
    pool_bd = jax.scipy.linalg.block_diag(*[pool_w[0, g] for g in range(len(POOL_WINDOWS))])
    return {
        "g1": norm1_g[0].reshape(1, D_MODEL), "g2": norm2_g[0].reshape(1, D_MODEL),
        "gf": normf_g.reshape(1, D_MODEL),
        "w_ada": w_ada[0], "b_ada": b_ada[0],
        "w_in": jnp.take(w_in[0], _qk_column_order(), axis=1).astype(BF16),
        "band_bias": jnp.asarray(_band_bias()),
        "pool_w": pool_bd.astype(BF16), "pool_scale": pool_scale[0].reshape(1, POOL_WIDTH),
        "w_out": w_out[0].astype(BF16), "w_router_t": w_router[0].T,
        "w_gate": w_gate[0], "w_up": w_up[0], "w_down": w_down[0],
    }


def kernel(x_prompt, x_sample, c_prompt, c_sample, norm1_g, norm2_g, normf_g, w_ada, b_ada, w_in, pool_w,
           pool_scale, w_out, w_router, w_gate, w_up, w_down):
    p = _layer_params(norm1_g, norm2_g, normf_g, w_ada, b_ada, w_in, pool_w, pool_scale, w_out, w_router,
                      w_gate, w_up, w_down)
    return _layer(((x_prompt, c_prompt), (x_sample, c_sample)), p)
```
